```python
import math
import jax, jax.numpy as jnp
from jax import lax
import numpy as np

D_MODEL = 2048
BATCH = 4
SEQ = 2048
DEPTH = 1
DEC_BATCH = 128
DEC_SEQ = 4
PAST_LEN = 16384
PAGE_SIZE = 128

D_MIX = D_MODEL
D_MLSTM = D_MIX // 2
N_HEADS = 4
HEAD_DIM = D_MLSTM // N_HEADS
D_CONV = D_MIX - D_MLSTM
CONV_WIDTH = 31
D_FF = 5632
CHUNK = 64
EPS = 1e-6
FFN_RES = 0.5
D_IN = 4 * D_MLSTM + 2 * D_CONV + 2 * N_HEADS

kernel_name = "hymba_mlstm_conformerconv_macaron_step"


def rmsnorm(x, g):
    xf = x.astype(jnp.float32)
    y = xf * lax.rsqrt(jnp.mean(xf * xf, axis=-1, keepdims=True) + EPS)
    return (y * g.astype(jnp.float32)).astype(x.dtype)


def layernorm(x, g, b):
    xf = x.astype(jnp.float32)
    mu = jnp.mean(xf, axis=-1, keepdims=True)
    var = jnp.mean(jnp.square(xf - mu), axis=-1, keepdims=True)
    y = (xf - mu) * lax.rsqrt(var + EPS)
    return (y * g.astype(jnp.float32) + b.astype(jnp.float32)).astype(x.dtype)


def swiglu(x, w_gate, w_up, w_down):
    return (jax.nn.silu(x @ w_gate) * (x @ w_up)) @ w_down


def mlstm_chunkwise(q, k, v, ig, lf, C0, n0, m0):
    B, T, H, Dh = q.shape
    L = math.gcd(T, CHUNK)
    nc = T // L

    def to_chunks(a):
        return jnp.moveaxis(a.reshape((B, nc, L) + a.shape[2:]), 1, 0)

    mask = jnp.tril(jnp.ones((L, L), dtype=bool))

    def step(carry, inp):
        C, n, m = carry
        qc, kc, vc, igc, lfc = inp
        bt = jnp.cumsum(lfc, axis=1).transpose(0, 2, 1)
        it = igc.transpose(0, 2, 1)
        D = bt[..., :, None] - bt[..., None, :] + it[..., None, :]
        D = jnp.where(mask, D, -jnp.inf)
        inter = bt + m[..., None]
        m_t = jnp.maximum(inter, jnp.max(D, axis=-1))
        w_intra = jnp.exp(D - m_t[..., None])
        w_inter = jnp.exp(inter - m_t)
        s = jnp.einsum('blhd,bshd->bhls', qc, kc) * w_intra
        num = (jnp.einsum('bhls,bshd->blhd', s, vc)
               + jnp.einsum('blhd,bhde->blhe', qc, C) * w_inter.transpose(0, 2, 1)[..., None])
        den = jnp.sum(s, axis=-1) + jnp.einsum('blhd,bhd->bhl', qc, n) * w_inter
        denom = jnp.maximum(jnp.abs(den), jnp.exp(-m_t))
        h = num / denom.transpose(0, 2, 1)[..., None]
        m_new = m_t[..., -1]
        g_state = jnp.exp(bt[..., -1] + m - m_new)
        g_keys = jnp.exp(bt[..., -1:] - bt + it - m_new[..., None])
        C_new = g_state[..., None, None] * C + jnp.einsum('bhs,bshd,bshe->bhde', g_keys, kc, vc)
        n_new = g_state[..., None] * n + jnp.einsum('bhs,bshd->bhd', g_keys, kc)
        return (C_new, n_new, m_new), h

    xs = (to_chunks(q), to_chunks(k), to_chunks(v), to_chunks(ig), to_chunks(lf))
    (C, n, m), hs = lax.scan(step, (C0, n0, m0), xs)
    h = jnp.moveaxis(hs, 0, 1).reshape(B, T, H, Dh)
    return h, C, n, m


def conformer_conv(a, b, conv_state, w_dw, b_dw, g_ln, b_ln):
    u = a * jax.nn.sigmoid(b)
    up = jnp.concatenate([conv_state.astype(u.dtype), u], axis=1)
    y = lax.conv_general_dilated(up, w_dw[:, None, :].astype(u.dtype), window_strides=(1,),
                                 padding='VALID', dimension_numbers=('NWC', 'WIO', 'NWC'),
                                 feature_group_count=D_CONV) + b_dw
    y = jax.nn.silu(layernorm(y, g_ln, b_ln))
    return y, up[:, -(CONV_WIDTH - 1):, :]


def layer(x, C0, n0, m0, conv0, p):
    B, T, _ = x.shape
    x = x + FFN_RES * rmsnorm(swiglu(rmsnorm(x, p['g_ffn1_pre']), p['w_ffn1_gate'], p['w_ffn1_up'],
                                     p['w_ffn1_down']), p['g_ffn1_post'])
    h = rmsnorm(x, p['g_mix_pre'])
    proj = h @ p['w_in']
    sizes = [D_MLSTM] * 4 + [D_CONV] * 2 + [N_HEADS] * 2
    cuts = np.cumsum(sizes)[:-1].tolist()
    q, k, v, o, ga, gb, ig, fg = jnp.split(proj, cuts, axis=-1)
    f32 = jnp.float32
    qh = q.astype(f32).reshape(B, T, N_HEADS, HEAD_DIM)
    kh = k.astype(f32).reshape(B, T, N_HEADS, HEAD_DIM) * (HEAD_DIM ** -0.5)
    vh = v.astype(f32).reshape(B, T, N_HEADS, HEAD_DIM)
    i_pre = ig.astype(f32) + p['b_igate'].astype(f32)
    log_f = jax.nn.log_sigmoid(fg.astype(f32) + p['b_fgate'].astype(f32))
    hm, C, n, m = mlstm_chunkwise(qh, kh, vh, i_pre, log_f, C0.astype(f32), n0.astype(f32), m0.astype(f32))
    hm = (jax.nn.sigmoid(o.astype(f32)) * hm.reshape(B, T, D_MLSTM)).astype(x.dtype)
    hc, conv_new = conformer_conv(ga, gb, conv0, p['w_dw'], p['b_dw'], p['g_conv_ln'], p['b_conv_ln'])
    mix = jnp.concatenate([hm, hc.astype(x.dtype)], axis=-1) @ p['w_out']
    x = x + rmsnorm(mix, p['g_mix_post'])
    x = x + FFN_RES * rmsnorm(swiglu(rmsnorm(x, p['g_ffn2_pre']), p['w_ffn2_gate'], p['w_ffn2_up'],
                                     p['w_ffn2_down']), p['g_ffn2_post'])
    return x, C, n, m, conv_new


def setup_inputs(seed: int = 0) -> dict:
    key = jax.random.key(seed)
    ks = iter(jax.random.split(key, 40))
    nrm = lambda shape, s: jax.random.normal(next(ks), shape, jnp.float32) * s
    gain = lambda shape: 1.0 + nrm(shape, 0.05)
    Lr = DEPTH
    b_f = jnp.broadcast_to(jnp.linspace(3.0, 6.0, N_HEADS, dtype=jnp.float32), (Lr, N_HEADS)) + nrm((Lr, N_HEADS), 0.01)
    return {
        "x_prompt": nrm((BATCH, SEQ, D_MODEL), 1.0),
        "x_sample": nrm((DEC_BATCH, DEC_SEQ, D_MODEL), 1.0),
        "state_mlstm_C": nrm((Lr, DEC_BATCH, N_HEADS, HEAD_DIM, HEAD_DIM), 0.1),
        "state_mlstm_n": nrm((Lr, DEC_BATCH, N_HEADS, HEAD_DIM), 0.1),
        "state_mlstm_m": nrm((Lr, DEC_BATCH, N_HEADS), 1.0),
        "state_conv": nrm((Lr, DEC_BATCH, CONV_WIDTH - 1, D_CONV), 0.5),
        "g_ffn1_pre": gain((Lr, D_MODEL)),
        "w_ffn1_gate": nrm((Lr, D_MODEL, D_FF), D_MODEL ** -0.5),
        "w_ffn1_up": nrm((Lr, D_MODEL, D_FF), D_MODEL ** -0.5),
        "w_ffn1_down": nrm((Lr, D_FF, D_MODEL), D_FF ** -0.5),
        "g_ffn1_post": gain((Lr, D_MODEL)),
        "g_mix_pre": gain((Lr, D_MODEL)),
        "w_in": nrm((Lr, D_MODEL, D_IN), D_MODEL ** -0.5),
        "b_igate": nrm((Lr, N_HEADS), 0.1),
        "b_fgate": b_f,
        "w_dw": nrm((Lr, CONV_WIDTH, D_CONV), CONV_WIDTH ** -0.5),
        "b_dw": nrm((Lr, D_CONV), 0.02),
        "g_conv_ln": gain((Lr, D_CONV)),
        "b_conv_ln": nrm((Lr, D_CONV), 0.02),
        "w_out": nrm((Lr, D_MIX, D_MODEL), D_MIX ** -0.5),
        "g_mix_post": gain((Lr, D_MODEL)),
        "g_ffn2_pre": gain((Lr, D_MODEL)),
        "w_ffn2_gate": nrm((Lr, D_MODEL, D_FF), D_MODEL ** -0.5),
        "w_ffn2_up": nrm((Lr, D_MODEL, D_FF), D_MODEL ** -0.5),
        "w_ffn2_down": nrm((Lr, D_FF, D_MODEL), D_FF ** -0.5),
        "g_ffn2_post": gain((Lr, D_MODEL)),
    }


def reference(x_prompt, x_sample, state_mlstm_C, state_mlstm_n, state_mlstm_m, state_conv,
              g_ffn1_pre, w_ffn1_gate, w_ffn1_up, w_ffn1_down, g_ffn1_post,
              g_mix_pre, w_in, b_igate, b_fgate, w_dw, b_dw, g_conv_ln, b_conv_ln, w_out, g_mix_post,
              g_ffn2_pre, w_ffn2_gate, w_ffn2_up, w_ffn2_down, g_ffn2_post):
    sdt = state_mlstm_C.dtype
    Bp = x_prompt.shape[0]
    xp, xs = x_prompt, x_sample
    Cp_l, np_l, mp_l, cp_l = [], [], [], []
    Cs_l, ns_l, ms_l, cs_l = [], [], [], []
    for l in range(DEPTH):
        p = {
            'g_ffn1_pre': g_ffn1_pre[l], 'w_ffn1_gate': w_ffn1_gate[l], 'w_ffn1_up': w_ffn1_up[l],
            'w_ffn1_down': w_ffn1_down[l], 'g_ffn1_post': g_ffn1_post[l],
            'g_mix_pre': g_mix_pre[l], 'w_in': w_in[l], 'b_igate': b_igate[l], 'b_fgate': b_fgate[l],
            'w_dw': w_dw[l], 'b_dw': b_dw[l], 'g_conv_ln': g_conv_ln[l], 'b_conv_ln': b_conv_ln[l],
            'w_out': w_out[l], 'g_mix_post': g_mix_post[l],
            'g_ffn2_pre': g_ffn2_pre[l], 'w_ffn2_gate': w_ffn2_gate[l], 'w_ffn2_up': w_ffn2_up[l],
            'w_ffn2_down': w_ffn2_down[l], 'g_ffn2_post': g_ffn2_post[l],
        }
        C0 = jnp.zeros((Bp, N_HEADS, HEAD_DIM, HEAD_DIM), jnp.float32)
        n0 = jnp.zeros((Bp, N_HEADS, HEAD_DIM), jnp.float32)
        m0 = jnp.zeros((Bp, N_HEADS), jnp.float32)
        conv0 = jnp.zeros((Bp, CONV_WIDTH - 1, D_CONV), xp.dtype)
        xp, Cp, npp, mp, cp = layer(xp, C0, n0, m0, conv0, p)
        xs, Cs, ns, ms, cs = layer(xs, state_mlstm_C[l], state_mlstm_n[l], state_mlstm_m[l], state_conv[l], p)
        Cp_l.append(Cp.astype(sdt)); np_l.append(npp.astype(sdt)); mp_l.append(mp.astype(sdt)); cp_l.append(cp.astype(sdt))
        Cs_l.append(Cs.astype(sdt)); ns_l.append(ns.astype(sdt)); ms_l.append(ms.astype(sdt)); cs_l.append(cs.astype(sdt))
    return (xp, xs,
            jnp.stack(Cp_l), jnp.stack(np_l), jnp.stack(mp_l), jnp.stack(cp_l),
            jnp.stack(Cs_l), jnp.stack(ns_l), jnp.stack(ms_l), jnp.stack(cs_l))
```

```python
import functools

import jax
import jax.numpy as jnp
from jax import lax
from jax.experimental import pallas as pl
from jax.experimental.pallas import tpu as pltpu

D_MODEL = 2048
N_HEADS = 4
HEAD_DIM = 256
D_MLSTM = N_HEADS * HEAD_DIM
D_CONV = D_MODEL - D_MLSTM
CONV_WIDTH = 31
HALO = CONV_WIDTH - 1
D_FF = 5632
D_MAIN = 4 * D_MLSTM + 2 * D_CONV
EPS = 1e-6
FFN_RES = 0.5
K_SCALE = HEAD_DIM ** -0.5

LANES = 128
ROWS = 128
VMEM_LIMIT = 56 * 1024 * 1024

F32 = jnp.float32
BF16 = jnp.bfloat16


def _params(sem):
    return pltpu.CompilerParams(dimension_semantics=sem, vmem_limit_bytes=VMEM_LIMIT)


def _rms(x, g):
    return x * lax.rsqrt(jnp.mean(x * x, axis=-1, keepdims=True) + EPS) * g


def _ffn_body(x_ref, gpre_ref, wg_ref, wu_ref, wd_ref, gpost_ref, o_ref, h_ref, acc_ref):
    f = pl.program_id(1)

    @pl.when(f == 0)
    def _():
        h_ref[...] = _rms(x_ref[...], gpre_ref[...]).astype(BF16)
        acc_ref[...] = jnp.zeros_like(acc_ref)

    h = h_ref[...]
    g = jnp.dot(h, wg_ref[...], preferred_element_type=F32)
    u = jnp.dot(h, wu_ref[...], preferred_element_type=F32)
    a = (g * jax.nn.sigmoid(g)) * u
    acc_ref[...] += jnp.dot(a.astype(BF16), wd_ref[...], preferred_element_type=F32)

    @pl.when(f == pl.num_programs(1) - 1)
    def _():
        o_ref[...] = x_ref[...] + FFN_RES * _rms(acc_ref[...], gpost_ref[...])


def _ffn(x, gpre, wg, wu, wd, gpost, *, tm=512, tf=512):
    m = x.shape[0]
    row = lambda i, f: (i, 0)
    const = lambda i, f: (0, 0)
    return pl.pallas_call(
        _ffn_body,
        grid=(m // tm, D_FF // tf),
        in_specs=[
            pl.BlockSpec((tm, D_MODEL), row),
            pl.BlockSpec((1, D_MODEL), const),
            pl.BlockSpec((D_MODEL, tf), lambda i, f: (0, f)),
            pl.BlockSpec((D_MODEL, tf), lambda i, f: (0, f)),
            pl.BlockSpec((tf, D_MODEL), lambda i, f: (f, 0)),
            pl.BlockSpec((1, D_MODEL), const),
        ],
        out_specs=pl.BlockSpec((tm, D_MODEL), row),
        out_shape=jax.ShapeDtypeStruct((m, D_MODEL), F32),
        scratch_shapes=[pltpu.VMEM((tm, D_MODEL), BF16), pltpu.VMEM((tm, D_MODEL), F32)],
        compiler_params=_params(("parallel", "arbitrary")),
        name="ffn",
    )(x, gpre, wg, wu, wd, gpost)


def _proj_in_body(x_ref, g_ref, w_ref, wgate_ref, proj_ref, gates_ref, h_ref):
    @pl.when(pl.program_id(1) == 0)
    def _():
        h = _rms(x_ref[...], g_ref[...]).astype(BF16)
        h_ref[...] = h
        gates_ref[...] = jnp.dot(h, wgate_ref[...], preferred_element_type=F32)

    proj_ref[...] = jnp.dot(h_ref[...], w_ref[...], preferred_element_type=F32)


def _proj_in(x, g, w_main, w_gate, *, tm=512, tn=1024):
    m = x.shape[0]
    return pl.pallas_call(
        _proj_in_body,
        grid=(m // tm, D_MAIN // tn),
        in_specs=[
            pl.BlockSpec((tm, D_MODEL), lambda i, n: (i, 0)),
            pl.BlockSpec((1, D_MODEL), lambda i, n: (0, 0)),
            pl.BlockSpec((D_MODEL, tn), lambda i, n: (0, n)),
            pl.BlockSpec((D_MODEL, LANES), lambda i, n: (0, 0)),
        ],
        out_specs=[
            pl.BlockSpec((tm, tn), lambda i, n: (i, n)),
            pl.BlockSpec((tm, LANES), lambda i, n: (i, 0)),
        ],
        out_shape=[
            jax.ShapeDtypeStruct((m, D_MAIN), F32),
            jax.ShapeDtypeStruct((m, LANES), F32),
        ],
        scratch_shapes=[pltpu.VMEM((tm, D_MODEL), BF16)],
        compiler_params=_params(("parallel", "arbitrary")),
        name="proj_in",
    )(x, g, w_main, w_gate)


def _log_sigmoid(x):
    return jnp.minimum(x, 0.0) - jnp.log1p(jnp.exp(-jnp.abs(x)))


def _seg_cumsum(x, seg_len):
    pos = lax.broadcasted_iota(jnp.int32, x.shape, 0) & (seg_len - 1)
    shift = 1
    while shift < seg_len:
        x = x + jnp.where(pos >= shift, pltpu.roll(x, shift, 0), 0.0)
        shift *= 2
    return x


def _pick_lane(x, j):
    lane = lax.broadcasted_iota(jnp.int32, x.shape, 1)
    return jnp.sum(jnp.where(lane == j, x, 0.0), axis=1, keepdims=True)


def _pick_row(x, j):
    sub = lax.broadcasted_iota(jnp.int32, x.shape, 0)
    return jnp.sum(jnp.where(sub == j, x, 0.0), axis=0, keepdims=True)


def _mlstm_tile(q, k, v, pre, hd, seg_len, m_prev):
    r = q.shape[0]
    log2 = seg_len.bit_length() - 1
    bt_all = _seg_cumsum(_log_sigmoid(pre), seg_len)
    ig_col = _pick_lane(pre, hd)
    bt_col = _pick_lane(bt_all, hd + N_HEADS)
    ig_row = _pick_row(pre.T, hd)
    bt_row = _pick_row(bt_all.T, hd + N_HEADS)

    t_idx = lax.broadcasted_iota(jnp.int32, (r, r), 0)
    s_idx = lax.broadcasted_iota(jnp.int32, (r, r), 1)
    same = (t_idx >> log2) == (s_idx >> log2)
    valid = same & (s_idx <= t_idx)
    last = s_idx == ((t_idx >> log2) << log2) + (seg_len - 1)

    key_w = ig_row - bt_row
    d = jnp.where(valid, bt_col + key_w, -jnp.inf)
    inter = bt_col + m_prev
    m_t = jnp.maximum(inter, jnp.max(d, axis=1, keepdims=True))
    w_intra = jnp.exp(d - m_t)
    w_inter = jnp.exp(inter - m_t)

    bt_last = jnp.sum(jnp.where(last, bt_row, 0.0), axis=1, keepdims=True)
    e = jnp.where(same, bt_last + key_w, -jnp.inf)
    m_end = jnp.maximum(bt_last + m_prev, jnp.max(e, axis=1, keepdims=True))
    g_keys = jnp.exp(bt_last - bt_col + ig_col - m_end)
    g_state = jnp.exp(bt_last + m_prev - m_end)

    ks = k * K_SCALE
    q_bf = q.astype(BF16)
    v_bf = v.astype(BF16)
    s = lax.dot_general(q_bf, ks.astype(BF16), (((1,), (1,)), ((), ())),
                        preferred_element_type=F32) * w_intra
    num = jnp.dot(s.astype(BF16), v_bf, preferred_element_type=F32)
    den = jnp.sum(s, axis=1, keepdims=True)
    kg = ks * g_keys
    return dict(q_bf=q_bf, v_bf=v_bf, num=num, den=den, kg=kg, m_t=m_t, w_inter=w_inter,
                g_state=g_state, m_end=m_end)


def _mlstm_out(t, q, o, q_c, q_n):
    num = t["num"] + q_c * t["w_inter"]
    den = t["den"] + q_n * t["w_inter"]
    h = num / jnp.maximum(jnp.abs(den), jnp.exp(-t["m_t"]))
    return (jax.nn.sigmoid(o) * h).astype(BF16)


def _mlstm_prompt_body(q_ref, k_ref, v_ref, o_ref, gates_ref, bias_ref,
                       hm_ref, c_out_ref, n_out_ref, m_out_ref, c_ref, n_ref, m_ref, *, n_chunks):
    hd = pl.program_id(1)
    tt = pl.program_id(2)

    @pl.when(tt == 0)
    def _():
        c_ref[...] = jnp.zeros_like(c_ref)
        n_ref[...] = jnp.zeros_like(n_ref)
        m_ref[...] = jnp.zeros_like(m_ref)

    for c in range(n_chunks):
        rows = pl.ds(c * ROWS, ROWS)
        q = q_ref[rows, :]
        pre = gates_ref[rows, :] + bias_ref[...]
        t = _mlstm_tile(q, k_ref[rows, :], v_ref[rows, :], pre, hd, ROWS, m_ref[:, 0:1])
        c_old = c_ref[...]
        n_old = n_ref[...]
        q_c = jnp.dot(t["q_bf"], c_old.astype(BF16), preferred_element_type=F32)
        q_n = jnp.sum(q * n_old, axis=1, keepdims=True)
        hm_ref[rows, :] = _mlstm_out(t, q, o_ref[rows, :], q_c, q_n)
        g = t["g_state"][0:1, :]
        c_ref[...] = g * c_old + lax.dot_general(
            t["kg"].astype(BF16), t["v_bf"], (((0,), (0,)), ((), ())), preferred_element_type=F32)
        n_ref[...] = g * n_old + jnp.sum(t["kg"], axis=0, keepdims=True)
        m_ref[...] = jnp.broadcast_to(t["m_end"][0:1, :], m_ref.shape)

    @pl.when(tt == pl.num_programs(2) - 1)
    def _():
        c_out_ref[0, 0] = c_ref[...]
        n_out_ref[0, 0] = n_ref[...]
        m_out_ref[0, 0] = m_ref[...]


def _mlstm_prompt(proj, gates, bias, batch, seq, *, tt=256):
    nt = seq // tt
    col = lambda j: (lambda b, h, t: (b * nt + t, j * N_HEADS + h))
    state = lambda b, h, t: (b, h, 0, 0)
    return pl.pallas_call(
        functools.partial(_mlstm_prompt_body, n_chunks=tt // ROWS),
        grid=(batch, N_HEADS, nt),
        in_specs=[
            pl.BlockSpec((tt, HEAD_DIM), col(0)),
            pl.BlockSpec((tt, HEAD_DIM), col(1)),
            pl.BlockSpec((tt, HEAD_DIM), col(2)),
            pl.BlockSpec((tt, HEAD_DIM), col(3)),
            pl.BlockSpec((tt, LANES), lambda b, h, t: (b * nt + t, 0)),
            pl.BlockSpec((1, LANES), lambda b, h, t: (0, 0)),
        ],
        out_specs=[
            pl.BlockSpec((tt, HEAD_DIM), lambda b, h, t: (b * nt + t, h)),
            pl.BlockSpec((1, 1, HEAD_DIM, HEAD_DIM), state),
            pl.BlockSpec((1, 1, 1, HEAD_DIM), state),
            pl.BlockSpec((1, 1, 1, LANES), state),
        ],
        out_shape=[
            jax.ShapeDtypeStruct((batch * seq, D_MLSTM), BF16),
            jax.ShapeDtypeStruct((batch, N_HEADS, HEAD_DIM, HEAD_DIM), F32),
            jax.ShapeDtypeStruct((batch, N_HEADS, 1, HEAD_DIM), F32),
            jax.ShapeDtypeStruct((batch, N_HEADS, 1, LANES), F32),
        ],
        scratch_shapes=[pltpu.VMEM((HEAD_DIM, HEAD_DIM), F32), pltpu.VMEM((1, HEAD_DIM), F32),
                        pltpu.VMEM((1, LANES), F32)],
        compiler_params=_params(("parallel", "parallel", "arbitrary")),
        name="mlstm_prompt",
    )(proj, proj, proj, proj, gates, bias)


def _mlstm_sample_body(q_ref, k_ref, v_ref, o_ref, gates_ref, bias_ref, mrow_ref, c_ref, n_ref,
                       hm_ref, c_out_ref, n_out_ref, m_out_ref, *, seg_len):
    hd = pl.program_id(1)
    n_seg = ROWS // seg_len
    grp = 16 // seg_len
    q = q_ref[...]
    pre = gates_ref[...] + bias_ref[...]
    t = _mlstm_tile(q, k_ref[...], v_ref[...], pre, hd, seg_len, mrow_ref[0])

    log2 = seg_len.bit_length() - 1
    seg_of_row = lax.broadcasted_iota(jnp.int32, (16, 1), 0) >> log2
    qc_parts, n_parts = [], []
    for j in range(ROWS // 16):
        qg = t["q_bf"][16 * j:16 * (j + 1)]
        qc, nr = None, None
        for i in range(grp):
            b = grp * j + i
            r = jnp.dot(qg, c_ref[0, b, 0].astype(BF16), preferred_element_type=F32)
            nb = jnp.broadcast_to(n_ref[b, 0], (16, HEAD_DIM))
            qc = r if i == 0 else jnp.where(seg_of_row == i, r, qc)
            nr = nb if i == 0 else jnp.where(seg_of_row == i, nb, nr)
        qc_parts.append(qc)
        n_parts.append(nr)
    q_c = jnp.concatenate(qc_parts, axis=0)
    q_n = jnp.sum(q * jnp.concatenate(n_parts, axis=0), axis=1, keepdims=True)
    hm_ref[...] = _mlstm_out(t, q, o_ref[...], q_c, q_n)

    kg = t["kg"]
    kg_t = kg.T
    seg_of_lane = lax.broadcasted_iota(jnp.int32, (1, ROWS), 1) >> log2
    seg_of_row8 = lax.broadcasted_iota(jnp.int32, (8, 1), 0) >> log2
    per8 = 8 // seg_len
    for b in range(n_seg):
        g = t["g_state"][seg_len * b:seg_len * b + 1, :]
        upd = jnp.dot(jnp.where(seg_of_lane == b, kg_t, 0.0).astype(BF16), t["v_bf"],
                      preferred_element_type=F32)
        c_out_ref[0, b, 0] = g * c_ref[0, b, 0] + upd
        kg8 = kg[8 * (b // per8):8 * (b // per8) + 8]
        n_out_ref[b, 0] = g * n_ref[b, 0] + jnp.sum(
            jnp.where(seg_of_row8 == (b % per8), kg8, 0.0), axis=0, keepdims=True)
    m_out_ref[0] = t["m_end"]


def _mlstm_sample(proj, gates, bias, m_rows, c0, n0, seg_len):
    m = proj.shape[0]
    n_seg = ROWS // seg_len
    col = lambda j: (lambda i, h: (i, j * N_HEADS + h))
    return pl.pallas_call(
        functools.partial(_mlstm_sample_body, seg_len=seg_len),
        grid=(m // ROWS, N_HEADS),
        in_specs=[
            pl.BlockSpec((ROWS, HEAD_DIM), col(0)),
            pl.BlockSpec((ROWS, HEAD_DIM), col(1)),
            pl.BlockSpec((ROWS, HEAD_DIM), col(2)),
            pl.BlockSpec((ROWS, HEAD_DIM), col(3)),
            pl.BlockSpec((ROWS, LANES), lambda i, h: (i, 0)),
            pl.BlockSpec((1, LANES), lambda i, h: (0, 0)),
            pl.BlockSpec((1, ROWS, 1), lambda i, h: (h, i, 0)),
            pl.BlockSpec((1, n_seg, 1, HEAD_DIM, HEAD_DIM), lambda i, h: (0, i, h, 0, 0)),
            pl.BlockSpec((n_seg, 1, 1, HEAD_DIM), lambda i, h: (i, h, 0, 0)),
        ],
        out_specs=[
            pl.BlockSpec((ROWS, HEAD_DIM), lambda i, h: (i, h)),
            pl.BlockSpec((1, n_seg, 1, HEAD_DIM, HEAD_DIM), lambda i, h: (0, i, h, 0, 0)),
            pl.BlockSpec((n_seg, 1, 1, HEAD_DIM), lambda i, h: (i, h, 0, 0)),
            pl.BlockSpec((1, ROWS, 1), lambda i, h: (h, i, 0)),
        ],
        out_shape=[
            jax.ShapeDtypeStruct((m, D_MLSTM), BF16),
            jax.ShapeDtypeStruct(c0.shape, F32),
            jax.ShapeDtypeStruct(n0.shape, F32),
            jax.ShapeDtypeStruct(m_rows.shape, F32),
        ],
        compiler_params=_params(("parallel", "parallel")),
        name="mlstm_sample",
    )(proj, proj, proj, proj, gates, bias, m_rows, c0, n0)


def _ln_swish(y, g, b):
    mu = jnp.mean(y, axis=-1, keepdims=True)
    yc = y - mu
    var = jnp.mean(yc * yc, axis=-1, keepdims=True)
    z = yc * lax.rsqrt(var + EPS) * g + b
    return z * jax.nn.sigmoid(z)


def _conv_prompt_body(a_ref, b_ref, w_ref, bdw_ref, gln_ref, bln_ref, hc_ref, st_ref, u_ref, y_ref, *, tt):
    t_id = pl.program_id(1)
    pad = 32

    @pl.when(t_id == 0)
    def _():
        u_ref[0:pad, :] = jnp.zeros((pad, D_CONV), F32)

    u_ref[pad:pad + tt, :] = a_ref[...] * jax.nn.sigmoid(b_ref[...])
    for rb in range(tt // ROWS):
        for cb in range(D_CONV // LANES):
            lanes = slice(cb * LANES, (cb + 1) * LANES)
            acc = jnp.zeros((ROWS, LANES), F32) + bdw_ref[:, lanes]
            for s in range(CONV_WIDTH):
                start = pad - HALO + rb * ROWS + s
                acc = acc + w_ref[s:s + 1, lanes] * u_ref[start:start + ROWS, lanes]
            y_ref[rb * ROWS:(rb + 1) * ROWS, lanes] = acc
    hc_ref[...] = _ln_swish(y_ref[...], gln_ref[...], bln_ref[...]).astype(BF16)

    @pl.when(t_id == pl.num_programs(1) - 1)
    def _():
        st_ref[0, 0] = u_ref[pad + tt - HALO:pad + tt, :]

    u_ref[0:pad, :] = u_ref[tt:tt + pad, :]


def _conv_prompt(proj, w_dw, b_dw, g_ln, b_ln, batch, seq, *, tt=256):
    nt = seq // tt
    const = lambda b, t: (0, 0)
    return pl.pallas_call(
        functools.partial(_conv_prompt_body, tt=tt),
        grid=(batch, nt),
        in_specs=[
            pl.BlockSpec((tt, D_CONV), lambda b, t: (b * nt + t, 4)),
            pl.BlockSpec((tt, D_CONV), lambda b, t: (b * nt + t, 5)),
            pl.BlockSpec((CONV_WIDTH, D_CONV), const),
            pl.BlockSpec((1, D_CONV), const),
            pl.BlockSpec((1, D_CONV), const),
            pl.BlockSpec((1, D_CONV), const),
        ],
        out_specs=[
            pl.BlockSpec((tt, D_CONV), lambda b, t: (b * nt + t, 0)),
            pl.BlockSpec((1, 1, HALO, D_CONV), lambda b, t: (0, b, 0, 0)),
        ],
        out_shape=[
            jax.ShapeDtypeStruct((batch * seq, D_CONV), BF16),
            jax.ShapeDtypeStruct((1, batch, HALO, D_CONV), F32),
        ],
        scratch_shapes=[pltpu.VMEM((tt + 32, D_CONV), F32), pltpu.VMEM((tt, D_CONV), F32)],
        compiler_params=_params(("parallel", "arbitrary")),
        name="conv_prompt",
    )(proj, proj, w_dw, b_dw, g_ln, b_ln)


def _conv_sample_body(a_ref, b_ref, st_ref, w_ref, bdw_ref, gln_ref, bln_ref, hc_ref, st_out_ref,
                      up_ref, y_ref, *, bb, seq):
    u = a_ref[...] * jax.nn.sigmoid(b_ref[...])
    w = w_ref[...]
    for b in range(bb):
        up_ref[0:HALO, :] = st_ref[0, b]
        up_ref[HALO:HALO + seq, :] = u[b * seq:(b + 1) * seq]
        acc = jnp.zeros((seq, D_CONV), F32) + bdw_ref[...]
        for s in range(CONV_WIDTH):
            acc = acc + w[s:s + 1, :] * up_ref[s:s + seq, :]
        y_ref[b * seq:(b + 1) * seq, :] = acc
        st_out_ref[0, b] = up_ref[seq:seq + HALO, :]
    hc_ref[...] = _ln_swish(y_ref[...], gln_ref[...], bln_ref[...]).astype(BF16)


def _conv_sample(proj, state, w_dw, b_dw, g_ln, b_ln, seq, *, bb=8):
    batch = state.shape[1]
    rows = bb * seq
    const = lambda i: (0, 0)
    return pl.pallas_call(
        functools.partial(_conv_sample_body, bb=bb, seq=seq),
        grid=(batch // bb,),
        in_specs=[
            pl.BlockSpec((rows, D_CONV), lambda i: (i, 4)),
            pl.BlockSpec((rows, D_CONV), lambda i: (i, 5)),
            pl.BlockSpec((1, bb, HALO, D_CONV), lambda i: (0, i, 0, 0)),
            pl.BlockSpec((CONV_WIDTH, D_CONV), const),
            pl.BlockSpec((1, D_CONV), const),
            pl.BlockSpec((1, D_CONV), const),
            pl.BlockSpec((1, D_CONV), const),
        ],
        out_specs=[
            pl.BlockSpec((rows, D_CONV), lambda i: (i, 0)),
            pl.BlockSpec((1, bb, HALO, D_CONV), lambda i: (0, i, 0, 0)),
        ],
        out_shape=[
            jax.ShapeDtypeStruct((batch * seq, D_CONV), BF16),
            jax.ShapeDtypeStruct(state.shape, F32),
        ],
        scratch_shapes=[pltpu.VMEM((HALO + seq + 6, D_CONV), F32), pltpu.VMEM((rows, D_CONV), F32)],
        compiler_params=_params(("parallel",)),
        name="conv_sample",
    )(proj, proj, state, w_dw, b_dw, g_ln, b_ln)


def _proj_out_body(hm_ref, hc_ref, x_ref, wa_ref, wb_ref, g_ref, o_ref):
    mix = (jnp.dot(hm_ref[...], wa_ref[...], preferred_element_type=F32)
           + jnp.dot(hc_ref[...], wb_ref[...], preferred_element_type=F32))
    o_ref[...] = x_ref[...] + _rms(mix, g_ref[...])


def _proj_out(hm, hc, x, w_out, g, *, tm=512):
    m = x.shape[0]
    row = lambda i: (i, 0)
    return pl.pallas_call(
        _proj_out_body,
        grid=(m // tm,),
        in_specs=[
            pl.BlockSpec((tm, D_MLSTM), row),
            pl.BlockSpec((tm, D_CONV), row),
            pl.BlockSpec((tm, D_MODEL), row),
            pl.BlockSpec((D_MLSTM, D_MODEL), lambda i: (0, 0)),
            pl.BlockSpec((D_CONV, D_MODEL), lambda i: (1, 0)),
            pl.BlockSpec((1, D_MODEL), lambda i: (0, 0)),
        ],
        out_specs=pl.BlockSpec((tm, D_MODEL), row),
        out_shape=jax.ShapeDtypeStruct((m, D_MODEL), F32),
        compiler_params=_params(("parallel",)),
        name="proj_out",
    )(hm, hc, x, w_out, w_out, g)


def kernel(x_prompt, x_sample, state_mlstm_C, state_mlstm_n, state_mlstm_m, state_conv, g_ffn1_pre, w_ffn1_gate, w_ffn1_up, w_ffn1_down, g_ffn1_post, g_mix_pre, w_in, b_igate, b_fgate, w_dw, b_dw, g_conv_ln, b_conv_ln, w_out, g_mix_post, g_ffn2_pre, w_ffn2_gate, w_ffn2_up, w_ffn2_down, g_ffn2_post):
    depth = state_mlstm_C.shape[0]
    assert depth == 1, "kernel handles a single layer"
    bp, tp, _ = x_prompt.shape
    bs, ts, _ = x_sample.shape
    l = 0

    bf = lambda w: w[l].astype(BF16)
    ffn1 = (g_ffn1_pre, bf(w_ffn1_gate), bf(w_ffn1_up), bf(w_ffn1_down), g_ffn1_post)
    ffn2 = (g_ffn2_pre, bf(w_ffn2_gate), bf(w_ffn2_up), bf(w_ffn2_down), g_ffn2_post)
    w_main = w_in[l, :, :D_MAIN].astype(BF16)
    w_gate = jnp.pad(w_in[l, :, D_MAIN:], ((0, 0), (0, LANES - 2 * N_HEADS))).astype(BF16)
    w_o = bf(w_out)
    bias = jnp.pad(jnp.concatenate([b_igate[l], b_fgate[l]]), (0, LANES - 2 * N_HEADS))[None, :]

    def pre_mix(x):
        x = _ffn(x, *ffn1)
        proj, gates = _proj_in(x, g_mix_pre, w_main, w_gate)
        return x, proj, gates

    def post_mix(x, hm, hc):
        x = _proj_out(hm, hc, x, w_o, g_mix_post)
        return _ffn(x, *ffn2)

    xp, proj_p, gates_p = pre_mix(x_prompt.reshape(bp * tp, D_MODEL))
    hm_p, c_p, n_p, m_p = _mlstm_prompt(proj_p, gates_p, bias, bp, tp)
    hc_p, conv_p = _conv_prompt(proj_p, w_dw[l], b_dw, g_conv_ln, b_conv_ln, bp, tp)
    yp = post_mix(xp, hm_p, hc_p).reshape(bp, tp, D_MODEL)

    xs, proj_s, gates_s = pre_mix(x_sample.reshape(bs * ts, D_MODEL))
    m_rows = jnp.repeat(state_mlstm_m[l].T, ts, axis=1)[:, :, None]
    hm_s, c_s, n_s, m_s = _mlstm_sample(
        proj_s, gates_s, bias, m_rows, state_mlstm_C, state_mlstm_n[l][:, :, None, :], ts)
    hc_s, conv_s = _conv_sample(proj_s, state_conv, w_dw[l], b_dw, g_conv_ln, b_conv_ln, ts)
    ys = post_mix(xs, hm_s, hc_s).reshape(bs, ts, D_MODEL)

    return (yp, ys,
            c_p[None], n_p[:, :, 0, :][None], m_p[:, :, 0, 0][None], conv_p,
            c_s, n_s[:, :, 0, :][None], m_s[:, ::ts, 0].T[None], conv_s)
```

```python
import functools

import jax
import jax.numpy as jnp
from jax import lax
from jax.experimental import pallas as pl
from jax.experimental.pallas import tpu as pltpu

D_MODEL = 2048
N_HEADS = 4
HEAD_DIM = 256
D_MLSTM = N_HEADS * HEAD_DIM
D_CONV = D_MODEL - D_MLSTM
CONV_WIDTH = 31
HALO = CONV_WIDTH - 1
CONV_PAD = 32
CONV_ROWS = 32
D_FF = 5632
D_MAIN = 4 * D_MLSTM + 2 * D_CONV
EPS = 1e-6
FFN_RES = 0.5
K_SCALE = HEAD_DIM ** -0.5

LANES = 128
ROWS = 128
VMEM_LIMIT = 56 * 1024 * 1024

F32 = jnp.float32
BF16 = jnp.bfloat16


def _params(sem):
    return pltpu.CompilerParams(dimension_semantics=sem, vmem_limit_bytes=VMEM_LIMIT)


def _rms(x, g):
    return x * lax.rsqrt(jnp.mean(x * x, axis=-1, keepdims=True) + EPS) * g


def _ffn_body(*refs, cast):
    if cast:
        x_ref, gpre_ref, wg_ref, wu_ref, wd_ref, gpost_ref, o_ref, wg_o, wu_o, wd_o, h_ref = refs
        wg, wu, wd = (r[...].astype(BF16) for r in (wg_ref, wu_ref, wd_ref))
        wg_o[...], wu_o[...], wd_o[...] = wg, wu, wd
    else:
        x_ref, gpre_ref, wg_ref, wu_ref, wd_ref, gpost_ref, o_ref, h_ref = refs
        wg, wu, wd = wg_ref[...], wu_ref[...], wd_ref[...]
    f = pl.program_id(1)

    @pl.when(f == 0)
    def _():
        h_ref[...] = _rms(x_ref[...], gpre_ref[...]).astype(BF16)
        o_ref[...] = jnp.zeros_like(o_ref)

    h = h_ref[...]
    g = jnp.dot(h, wg, preferred_element_type=F32)
    u = jnp.dot(h, wu, preferred_element_type=F32)
    a = ((g * jax.nn.sigmoid(g)) * u).astype(BF16)
    o_ref[...] += jnp.dot(a, wd, preferred_element_type=F32)

    @pl.when(f == pl.num_programs(1) - 1)
    def _():
        o_ref[...] = x_ref[...] + FFN_RES * _rms(o_ref[...], gpost_ref[...])


def _ffn(x, gpre, wg, wu, wd, gpost):
    m = x.shape[0]
    cast = wg.dtype == F32
    tm = min(m, 512)
    tf = 256 if cast else 512
    row = lambda i, f: (i, 0)
    const = lambda i, f: (0, 0)
    w_specs = [
        pl.BlockSpec((D_MODEL, tf), lambda i, f: (0, f)),
        pl.BlockSpec((D_MODEL, tf), lambda i, f: (0, f)),
        pl.BlockSpec((tf, D_MODEL), lambda i, f: (f, 0)),
    ]
    out_specs = [pl.BlockSpec((tm, D_MODEL), row)]
    out_shape = [jax.ShapeDtypeStruct((m, D_MODEL), F32)]
    if cast:
        assert m == tm, "the casting variant writes each weight tile once"
        out_specs += w_specs
        out_shape += [jax.ShapeDtypeStruct(w.shape, BF16) for w in (wg, wu, wd)]
    return pl.pallas_call(
        functools.partial(_ffn_body, cast=cast),
        grid=(m // tm, D_FF // tf),
        in_specs=[pl.BlockSpec((tm, D_MODEL), row), pl.BlockSpec((1, D_MODEL), const)] + w_specs
        + [pl.BlockSpec((1, D_MODEL), const)],
        out_specs=out_specs,
        out_shape=out_shape,
        scratch_shapes=[pltpu.VMEM((tm, D_MODEL), BF16)],
        compiler_params=_params(("parallel", "arbitrary")),
        name="ffn_cast" if cast else "ffn",
    )(x, gpre, wg, wu, wd, gpost)


def _proj_in_body(*refs, cast):
    if cast:
        x_ref, g_ref, w_ref, wgate_ref, proj_ref, gates_ref, w_o, h_ref = refs
        w = w_ref[0].astype(BF16)
        w_o[...] = w
    else:
        x_ref, g_ref, w_ref, wgate_ref, proj_ref, gates_ref, h_ref = refs
        w = w_ref[...]

    @pl.when(pl.program_id(1) == 0)
    def _():
        h = _rms(x_ref[...], g_ref[...]).astype(BF16)
        h_ref[...] = h
        gates_ref[...] = jnp.dot(h, wgate_ref[...], preferred_element_type=F32)

    proj_ref[...] = jnp.dot(h_ref[...], w, preferred_element_type=F32)


def _proj_in(x, g, w_main, w_gate, *, tn=1024):
    m = x.shape[0]
    cast = w_main.dtype == F32
    tm = min(m, 1024)
    if cast:
        assert m == tm, "the casting variant writes each weight tile once"
        w_spec = pl.BlockSpec((1, D_MODEL, tn), lambda i, n: (0, 0, n))
    else:
        w_spec = pl.BlockSpec((D_MODEL, tn), lambda i, n: (0, n))
    out_specs = [pl.BlockSpec((tm, tn), lambda i, n: (i, n)), pl.BlockSpec((tm, LANES), lambda i, n: (i, 0))]
    out_shape = [jax.ShapeDtypeStruct((m, D_MAIN), F32), jax.ShapeDtypeStruct((m, LANES), F32)]
    if cast:
        out_specs.append(pl.BlockSpec((D_MODEL, tn), lambda i, n: (0, n)))
        out_shape.append(jax.ShapeDtypeStruct((D_MODEL, D_MAIN), BF16))
    return pl.pallas_call(
        functools.partial(_proj_in_body, cast=cast),
        grid=(m // tm, D_MAIN // tn),
        in_specs=[
            pl.BlockSpec((tm, D_MODEL), lambda i, n: (i, 0)),
            pl.BlockSpec((1, D_MODEL), lambda i, n: (0, 0)),
            w_spec,
            pl.BlockSpec((D_MODEL, LANES), lambda i, n: (0, 0)),
        ],
        out_specs=out_specs,
        out_shape=out_shape,
        scratch_shapes=[pltpu.VMEM((tm, D_MODEL), BF16)],
        compiler_params=_params(("parallel", "arbitrary")),
        name="proj_in_cast" if cast else "proj_in",
    )(x, g, w_main, w_gate)


def _log_sigmoid(x):
    return jnp.minimum(x, 0.0) - jnp.log1p(jnp.exp(-jnp.abs(x)))


def _seg_cumsum(x, seg_len):
    pos = lax.broadcasted_iota(jnp.int32, x.shape, 0) & (seg_len - 1)
    shift = 1
    while shift < seg_len:
        x = x + jnp.where(pos >= shift, pltpu.roll(x, shift, 0), 0.0)
        shift *= 2
    return x


def _gate_terms(pre, seg_len):
    bt = _seg_cumsum(_log_sigmoid(pre), seg_len)
    return pre, bt, pre.T, bt.T


def _tile_masks(seg_len):
    log2 = seg_len.bit_length() - 1
    t_idx = lax.broadcasted_iota(jnp.int32, (ROWS, ROWS), 0)
    s_idx = lax.broadcasted_iota(jnp.int32, (ROWS, ROWS), 1)
    if seg_len == ROWS:
        return s_idx <= t_idx, None, None
    same = (t_idx >> log2) == (s_idx >> log2)
    last = s_idx == ((t_idx >> log2) << log2) + (seg_len - 1)
    return same & (s_idx <= t_idx), same, last


def _pick(x, j, axis):
    if isinstance(j, int):
        return x[:, j:j + 1] if axis == 1 else x[j:j + 1, :]
    idx = lax.broadcasted_iota(jnp.int32, x.shape, axis)
    return jnp.sum(jnp.where(idx == j, x, 0.0), axis=axis, keepdims=True)


def _mlstm_tile(q, k, v, gates, masks, hd, seg_len, m_prev):
    pre, bt_all, pre_t, bt_t = gates
    valid, same, last = masks
    ig_col = _pick(pre, hd, 1)
    bt_col = _pick(bt_all, hd + N_HEADS, 1)
    key_w = _pick(pre_t, hd, 0) - _pick(bt_t, hd + N_HEADS, 0)

    d = jnp.where(valid, bt_col + key_w, -jnp.inf)
    inter = bt_col + m_prev
    m_t = jnp.maximum(inter, jnp.max(d, axis=1, keepdims=True))
    w_intra = jnp.exp(d - m_t)
    w_inter = jnp.exp(inter - m_t)

    if seg_len == ROWS:
        bt_last = bt_col[ROWS - 1:ROWS, :]
        m_end = m_t[ROWS - 1:ROWS, :]
    else:
        bt_row = _pick(bt_t, hd + N_HEADS, 0)
        bt_last = jnp.sum(jnp.where(last, bt_row, 0.0), axis=1, keepdims=True)
        e = jnp.where(same, bt_last + key_w, -jnp.inf)
        m_end = jnp.maximum(bt_last + m_prev, jnp.max(e, axis=1, keepdims=True))
    g_keys = jnp.exp(bt_last - bt_col + ig_col - m_end)
    g_state = jnp.exp(bt_last + m_prev - m_end)

    ks = k * K_SCALE
    q_bf = q.astype(BF16)
    v_bf = v.astype(BF16)
    s = lax.dot_general(q_bf, ks.astype(BF16), (((1,), (1,)), ((), ())),
                        preferred_element_type=F32) * w_intra
    num = jnp.dot(s.astype(BF16), v_bf, preferred_element_type=F32)
    den = jnp.sum(s, axis=1, keepdims=True)
    kg = ks * g_keys
    return dict(q_bf=q_bf, v_bf=v_bf, num=num, den=den, kg=kg, m_t=m_t, w_inter=w_inter,
                g_state=g_state, m_end=m_end)


def _mlstm_out(t, q, o, q_c, q_n):
    num = t["num"] + q_c * t["w_inter"]
    den = t["den"] + q_n * t["w_inter"]
    h = num / jnp.maximum(jnp.abs(den), jnp.exp(-t["m_t"]))
    return (jax.nn.sigmoid(o) * h).astype(BF16)


def _mlstm_prompt_body(q_ref, k_ref, v_ref, o_ref, gates_ref, bias_ref,
                       hm_ref, c_out_ref, n_out_ref, m_out_ref, c_ref, n_ref, m_ref, *, n_chunks):
    tt = pl.program_id(1)

    @pl.when(tt == 0)
    def _():
        c_ref[...] = jnp.zeros_like(c_ref)
        n_ref[...] = jnp.zeros_like(n_ref)
        m_ref[...] = jnp.zeros_like(m_ref)

    masks = _tile_masks(ROWS)
    for c in range(n_chunks):
        rows = pl.ds(c * ROWS, ROWS)
        gates = _gate_terms(gates_ref[rows, :] + bias_ref[...], ROWS)
        for hd in range(N_HEADS):
            cols = pl.ds(hd * HEAD_DIM, HEAD_DIM)
            q = q_ref[rows, cols]
            t = _mlstm_tile(q, k_ref[rows, cols], v_ref[rows, cols], gates, masks, hd, ROWS,
                            m_ref[hd][:, 0:1])
            c_old = c_ref[hd]
            n_old = n_ref[hd]
            q_c = jnp.dot(t["q_bf"], c_old.astype(BF16), preferred_element_type=F32)
            q_n = jnp.sum(q * n_old, axis=1, keepdims=True)
            hm_ref[rows, cols] = _mlstm_out(t, q, o_ref[rows, cols], q_c, q_n)
            g = t["g_state"]
            c_ref[hd] = g * c_old + lax.dot_general(
                t["kg"].astype(BF16), t["v_bf"], (((0,), (0,)), ((), ())), preferred_element_type=F32)
            n_ref[hd] = g * n_old + jnp.sum(t["kg"], axis=0, keepdims=True)
            m_ref[hd] = jnp.broadcast_to(t["m_end"], (1, LANES))

    @pl.when(tt == pl.num_programs(1) - 1)
    def _():
        c_out_ref[0] = c_ref[...]
        n_out_ref[0] = n_ref[...]
        m_out_ref[0] = m_ref[...]


def _mlstm_prompt(proj, gates, bias, batch, seq, *, tt=256):
    nt = seq // tt
    col = lambda j: (lambda b, t: (b * nt + t, j))
    state = lambda b, t: (b, 0, 0, 0)
    return pl.pallas_call(
        functools.partial(_mlstm_prompt_body, n_chunks=tt // ROWS),
        grid=(batch, nt),
        in_specs=[
            pl.BlockSpec((tt, D_MLSTM), col(0)),
            pl.BlockSpec((tt, D_MLSTM), col(1)),
            pl.BlockSpec((tt, D_MLSTM), col(2)),
            pl.BlockSpec((tt, D_MLSTM), col(3)),
            pl.BlockSpec((tt, LANES), lambda b, t: (b * nt + t, 0)),
            pl.BlockSpec((1, LANES), lambda b, t: (0, 0)),
        ],
        out_specs=[
            pl.BlockSpec((tt, D_MLSTM), lambda b, t: (b * nt + t, 0)),
            pl.BlockSpec((1, N_HEADS, HEAD_DIM, HEAD_DIM), state),
            pl.BlockSpec((1, N_HEADS, 1, HEAD_DIM), state),
            pl.BlockSpec((1, N_HEADS, 1, LANES), state),
        ],
        out_shape=[
            jax.ShapeDtypeStruct((batch * seq, D_MLSTM), BF16),
            jax.ShapeDtypeStruct((batch, N_HEADS, HEAD_DIM, HEAD_DIM), F32),
            jax.ShapeDtypeStruct((batch, N_HEADS, 1, HEAD_DIM), F32),
            jax.ShapeDtypeStruct((batch, N_HEADS, 1, LANES), F32),
        ],
        scratch_shapes=[pltpu.VMEM((N_HEADS, HEAD_DIM, HEAD_DIM), F32), pltpu.VMEM((N_HEADS, 1, HEAD_DIM), F32),
                        pltpu.VMEM((N_HEADS, 1, LANES), F32)],
        compiler_params=_params(("parallel", "arbitrary")),
        name="mlstm_prompt",
    )(proj, proj, proj, proj, gates, bias)


def _mlstm_sample_body(q_ref, k_ref, v_ref, o_ref, gates_ref, bias_ref, mrow_ref, c_ref, n_ref,
                       hm_ref, c_out_ref, n_out_ref, m_out_ref, *, seg_len):
    hd = pl.program_id(1)
    n_seg = ROWS // seg_len
    grp = 16 // seg_len
    log2 = seg_len.bit_length() - 1
    q = q_ref[...]
    gates = _gate_terms(gates_ref[...] + bias_ref[...], seg_len)
    t = _mlstm_tile(q, k_ref[...], v_ref[...], gates, _tile_masks(seg_len), hd, seg_len, mrow_ref[0])

    seg_of_row = lax.broadcasted_iota(jnp.int32, (16, 1), 0) >> log2
    qc_parts, n_parts = [], []
    for j in range(ROWS // 16):
        qg = t["q_bf"][16 * j:16 * (j + 1)]
        qc, nr = None, None
        for i in range(grp):
            b = grp * j + i
            r = jnp.dot(qg, c_ref[0, b, 0].astype(BF16), preferred_element_type=F32)
            nb = jnp.broadcast_to(n_ref[b, 0], (16, HEAD_DIM))
            qc = r if i == 0 else jnp.where(seg_of_row == i, r, qc)
            nr = nb if i == 0 else jnp.where(seg_of_row == i, nb, nr)
        qc_parts.append(qc)
        n_parts.append(nr)
    q_c = jnp.concatenate(qc_parts, axis=0)
    q_n = jnp.sum(q * jnp.concatenate(n_parts, axis=0), axis=1, keepdims=True)
    hm_ref[...] = _mlstm_out(t, q, o_ref[...], q_c, q_n)

    kg = t["kg"]
    kg_t = kg.T
    seg_of_lane = lax.broadcasted_iota(jnp.int32, (1, ROWS), 1) >> log2
    seg_of_row8 = lax.broadcasted_iota(jnp.int32, (8, 1), 0) >> log2
    per8 = 8 // seg_len
    for b in range(n_seg):
        g = t["g_state"][seg_len * b:seg_len * b + 1, :]
        upd = jnp.dot(jnp.where(seg_of_lane == b, kg_t, 0.0).astype(BF16), t["v_bf"],
                      preferred_element_type=F32)
        c_out_ref[0, b, 0] = g * c_ref[0, b, 0] + upd
        kg8 = kg[8 * (b // per8):8 * (b // per8) + 8]
        n_out_ref[b, 0] = g * n_ref[b, 0] + jnp.sum(
            jnp.where(seg_of_row8 == (b % per8), kg8, 0.0), axis=0, keepdims=True)
    m_out_ref[0] = t["m_end"]


def _mlstm_sample(proj, gates, bias, m_rows, c0, n0, seg_len):
    m = proj.shape[0]
    n_seg = ROWS // seg_len
    col = lambda j: (lambda i, h: (i, j * N_HEADS + h))
    return pl.pallas_call(
        functools.partial(_mlstm_sample_body, seg_len=seg_len),
        grid=(m // ROWS, N_HEADS),
        in_specs=[
            pl.BlockSpec((ROWS, HEAD_DIM), col(0)),
            pl.BlockSpec((ROWS, HEAD_DIM), col(1)),
            pl.BlockSpec((ROWS, HEAD_DIM), col(2)),
            pl.BlockSpec((ROWS, HEAD_DIM), col(3)),
            pl.BlockSpec((ROWS, LANES), lambda i, h: (i, 0)),
            pl.BlockSpec((1, LANES), lambda i, h: (0, 0)),
            pl.BlockSpec((1, ROWS, 1), lambda i, h: (h, i, 0)),
            pl.BlockSpec((1, n_seg, 1, HEAD_DIM, HEAD_DIM), lambda i, h: (0, i, h, 0, 0)),
            pl.BlockSpec((n_seg, 1, 1, HEAD_DIM), lambda i, h: (i, h, 0, 0)),
        ],
        out_specs=[
            pl.BlockSpec((ROWS, HEAD_DIM), lambda i, h: (i, h)),
            pl.BlockSpec((1, n_seg, 1, HEAD_DIM, HEAD_DIM), lambda i, h: (0, i, h, 0, 0)),
            pl.BlockSpec((n_seg, 1, 1, HEAD_DIM), lambda i, h: (i, h, 0, 0)),
            pl.BlockSpec((1, ROWS, 1), lambda i, h: (h, i, 0)),
        ],
        out_shape=[
            jax.ShapeDtypeStruct((m, D_MLSTM), BF16),
            jax.ShapeDtypeStruct(c0.shape, F32),
            jax.ShapeDtypeStruct(n0.shape, F32),
            jax.ShapeDtypeStruct(m_rows.shape, F32),
        ],
        compiler_params=_params(("parallel", "parallel")),
        name="mlstm_sample",
    )(proj, proj, proj, proj, gates, bias, m_rows, c0, n0)


def _ln_swish(y, g, b):
    mu = jnp.mean(y, axis=-1, keepdims=True)
    yc = y - mu
    var = jnp.mean(yc * yc, axis=-1, keepdims=True)
    z = yc * lax.rsqrt(var + EPS) * g + b
    return z * jax.nn.sigmoid(z)


def _conv_prompt_body(a_ref, b_ref, w_ref, bdw_ref, gln_ref, bln_ref, hc_ref, st_ref,
                      u_ref, us_ref, wb_ref, y_ref, *, tt):
    t_id = pl.program_id(1)
    n_shift = tt + CONV_PAD - 8

    @pl.when(t_id == 0)
    def _():
        u_ref[0:CONV_PAD, :] = jnp.zeros((CONV_PAD, D_CONV), F32)
        for s in range(CONV_WIDTH):
            wb_ref[s] = jnp.broadcast_to(w_ref[s:s + 1, :], (8, D_CONV))

    u_ref[CONV_PAD:CONV_PAD + tt, :] = a_ref[...] * jax.nn.sigmoid(b_ref[...])
    for r in range(1, 8):
        us_ref[r - 1] = u_ref[r:r + n_shift, :]

    def row_block(i, carry):
        base = pl.multiple_of(i * CONV_ROWS, CONV_ROWS)
        n_slab = CONV_ROWS // 8
        acc = [jnp.broadcast_to(bdw_ref[...], (8, D_CONV))] * n_slab
        for s in range(CONV_WIDTH):
            k8, r = divmod(CONV_PAD - HALO + s, 8)
            w = wb_ref[s]
            for j in range(n_slab):
                rows = pl.ds(pl.multiple_of(base + 8 * (k8 + j), 8), 8)
                win = u_ref[rows, :] if r == 0 else us_ref[r - 1, rows, :]
                acc[j] = acc[j] + w * win
        for j in range(n_slab):
            y_ref[pl.ds(pl.multiple_of(base + 8 * j, 8), 8), :] = acc[j]
        return carry

    lax.fori_loop(0, tt // CONV_ROWS, row_block, 0)
    hc_ref[...] = _ln_swish(y_ref[...], gln_ref[...], bln_ref[...]).astype(BF16)

    @pl.when(t_id == pl.num_programs(1) - 1)
    def _():
        st_ref[0, 0] = u_ref[CONV_PAD + tt - HALO:CONV_PAD + tt, :]

    u_ref[0:CONV_PAD, :] = u_ref[tt:tt + CONV_PAD, :]


def _conv_prompt(proj, w_dw, b_dw, g_ln, b_ln, batch, seq, *, tt=256):
    nt = seq // tt
    const = lambda b, t: (0, 0)
    return pl.pallas_call(
        functools.partial(_conv_prompt_body, tt=tt),
        grid=(batch, nt),
        in_specs=[
            pl.BlockSpec((tt, D_CONV), lambda b, t: (b * nt + t, 4)),
            pl.BlockSpec((tt, D_CONV), lambda b, t: (b * nt + t, 5)),
            pl.BlockSpec((CONV_WIDTH, D_CONV), const),
            pl.BlockSpec((1, D_CONV), const),
            pl.BlockSpec((1, D_CONV), const),
            pl.BlockSpec((1, D_CONV), const),
        ],
        out_specs=[
            pl.BlockSpec((tt, D_CONV), lambda b, t: (b * nt + t, 0)),
            pl.BlockSpec((1, 1, HALO, D_CONV), lambda b, t: (0, b, 0, 0)),
        ],
        out_shape=[
            jax.ShapeDtypeStruct((batch * seq, D_CONV), BF16),
            jax.ShapeDtypeStruct((1, batch, HALO, D_CONV), F32),
        ],
        scratch_shapes=[pltpu.VMEM((tt + CONV_PAD, D_CONV), F32),
                        pltpu.VMEM((7, tt + CONV_PAD - 8, D_CONV), F32),
                        pltpu.VMEM((CONV_WIDTH, 8, D_CONV), F32),
                        pltpu.VMEM((tt, D_CONV), F32)],
        compiler_params=_params(("parallel", "arbitrary")),
        name="conv_prompt",
    )(proj, proj, w_dw, b_dw, g_ln, b_ln)


def _conv_sample_body(a_ref, b_ref, st_ref, w_ref, bdw_ref, gln_ref, bln_ref, hc_ref, st_out_ref,
                      up_ref, y_ref, *, bb, seq):
    u = a_ref[...] * jax.nn.sigmoid(b_ref[...])
    w = w_ref[...]
    for b in range(bb):
        up_ref[0:HALO, :] = st_ref[0, b]
        up_ref[HALO:HALO + seq, :] = u[b * seq:(b + 1) * seq]
        acc = jnp.zeros((seq, D_CONV), F32) + bdw_ref[...]
        for s in range(CONV_WIDTH):
            acc = acc + w[s:s + 1, :] * up_ref[s:s + seq, :]
        y_ref[b * seq:(b + 1) * seq, :] = acc
        st_out_ref[0, b] = up_ref[seq:seq + HALO, :]
    hc_ref[...] = _ln_swish(y_ref[...], gln_ref[...], bln_ref[...]).astype(BF16)


def _conv_sample(proj, state, w_dw, b_dw, g_ln, b_ln, seq, *, bb=8):
    batch = state.shape[1]
    rows = bb * seq
    const = lambda i: (0, 0)
    return pl.pallas_call(
        functools.partial(_conv_sample_body, bb=bb, seq=seq),
        grid=(batch // bb,),
        in_specs=[
            pl.BlockSpec((rows, D_CONV), lambda i: (i, 4)),
            pl.BlockSpec((rows, D_CONV), lambda i: (i, 5)),
            pl.BlockSpec((1, bb, HALO, D_CONV), lambda i: (0, i, 0, 0)),
            pl.BlockSpec((CONV_WIDTH, D_CONV), const),
            pl.BlockSpec((1, D_CONV), const),
            pl.BlockSpec((1, D_CONV), const),
            pl.BlockSpec((1, D_CONV), const),
        ],
        out_specs=[
            pl.BlockSpec((rows, D_CONV), lambda i: (i, 0)),
            pl.BlockSpec((1, bb, HALO, D_CONV), lambda i: (0, i, 0, 0)),
        ],
        out_shape=[
            jax.ShapeDtypeStruct((batch * seq, D_CONV), BF16),
            jax.ShapeDtypeStruct(state.shape, F32),
        ],
        scratch_shapes=[pltpu.VMEM((HALO + seq + 6, D_CONV), F32), pltpu.VMEM((rows, D_CONV), F32)],
        compiler_params=_params(("parallel",)),
        name="conv_sample",
    )(proj, proj, state, w_dw, b_dw, g_ln, b_ln)


def _proj_out_body(hm_ref, hc_ref, x_ref, wa_ref, wb_ref, g_ref, o_ref):
    mix = (jnp.dot(hm_ref[...], wa_ref[...], preferred_element_type=F32)
           + jnp.dot(hc_ref[...], wb_ref[...], preferred_element_type=F32))
    o_ref[...] = x_ref[...] + _rms(mix, g_ref[...])


def _proj_out(hm, hc, x, w_out, g, *, tm=512):
    m = x.shape[0]
    row = lambda i: (i, 0)
    return pl.pallas_call(
        _proj_out_body,
        grid=(m // tm,),
        in_specs=[
            pl.BlockSpec((tm, D_MLSTM), row),
            pl.BlockSpec((tm, D_CONV), row),
            pl.BlockSpec((tm, D_MODEL), row),
            pl.BlockSpec((D_MLSTM, D_MODEL), lambda i: (0, 0)),
            pl.BlockSpec((D_CONV, D_MODEL), lambda i: (1, 0)),
            pl.BlockSpec((1, D_MODEL), lambda i: (0, 0)),
        ],
        out_specs=pl.BlockSpec((tm, D_MODEL), row),
        out_shape=jax.ShapeDtypeStruct((m, D_MODEL), F32),
        compiler_params=_params(("parallel",)),
        name="proj_out",
    )(hm, hc, x, w_out, w_out, g)


def kernel(x_prompt, x_sample, state_mlstm_C, state_mlstm_n, state_mlstm_m, state_conv, g_ffn1_pre, w_ffn1_gate, w_ffn1_up, w_ffn1_down, g_ffn1_post, g_mix_pre, w_in, b_igate, b_fgate, w_dw, b_dw, g_conv_ln, b_conv_ln, w_out, g_mix_post, g_ffn2_pre, w_ffn2_gate, w_ffn2_up, w_ffn2_down, g_ffn2_post):
    depth = state_mlstm_C.shape[0]
    assert depth == 1, "kernel handles a single layer"
    bp, tp, _ = x_prompt.shape
    bs, ts, _ = x_sample.shape
    l = 0

    w_gate = jnp.pad(w_in[l, :, D_MAIN:], ((0, 0), (0, LANES - 2 * N_HEADS))).astype(BF16)
    w_o = w_out[l].astype(BF16)
    bias = jnp.pad(jnp.concatenate([b_igate[l], b_fgate[l]]), (0, LANES - 2 * N_HEADS))[None, :]

    xs, *ffn1 = _ffn(x_sample.reshape(bs * ts, D_MODEL), g_ffn1_pre,
                     w_ffn1_gate[l], w_ffn1_up[l], w_ffn1_down[l], g_ffn1_post)
    proj_s, gates_s, w_main = _proj_in(xs, g_mix_pre, w_in, w_gate)
    m_rows = jnp.repeat(state_mlstm_m[l].T, ts, axis=1)[:, :, None]
    hm_s, c_s, n_s, m_s = _mlstm_sample(
        proj_s, gates_s, bias, m_rows, state_mlstm_C, state_mlstm_n[l][:, :, None, :], ts)
    hc_s, conv_s = _conv_sample(proj_s, state_conv, w_dw[l], b_dw, g_conv_ln, b_conv_ln, ts)
    xs = _proj_out(hm_s, hc_s, xs, w_o, g_mix_post)
    ys, *ffn2 = _ffn(xs, g_ffn2_pre, w_ffn2_gate[l], w_ffn2_up[l], w_ffn2_down[l], g_ffn2_post)
    ys = ys.reshape(bs, ts, D_MODEL)

    xp = _ffn(x_prompt.reshape(bp * tp, D_MODEL), g_ffn1_pre, *ffn1, g_ffn1_post)[0]
    proj_p, gates_p = _proj_in(xp, g_mix_pre, w_main, w_gate)
    hm_p, c_p, n_p, m_p = _mlstm_prompt(proj_p, gates_p, bias, bp, tp)
    hc_p, conv_p = _conv_prompt(proj_p, w_dw[l], b_dw, g_conv_ln, b_conv_ln, bp, tp)
    xp = _proj_out(hm_p, hc_p, xp, w_o, g_mix_post)
    yp = _ffn(xp, g_ffn2_pre, *ffn2, g_ffn2_post)[0].reshape(bp, tp, D_MODEL)

    return (yp, ys,
            c_p[None], n_p[:, :, 0, :][None], m_p[:, :, 0, 0][None], conv_p,
            c_s, n_s[:, :, 0, :][None], m_s[:, ::ts, 0].T[None], conv_s)
```

```python
import functools

import jax
import jax.numpy as jnp
from jax import lax
from jax.experimental import pallas as pl
from jax.experimental.pallas import tpu as pltpu

D_MODEL = 2048
N_HEADS = 4
HEAD_DIM = 256
D_MLSTM = N_HEADS * HEAD_DIM
D_CONV = D_MODEL - D_MLSTM
CONV_WIDTH = 31
HALO = CONV_WIDTH - 1
CONV_PAD = 32
CONV_ROWS = 32
D_FF = 5632
D_MAIN = 4 * D_MLSTM + 2 * D_CONV
EPS = 1e-6
FFN_RES = 0.5
K_SCALE = HEAD_DIM ** -0.5

LANES = 128
ROWS = 128
VMEM_LIMIT = 56 * 1024 * 1024

F32 = jnp.float32
BF16 = jnp.bfloat16


def _params(sem):
    return pltpu.CompilerParams(dimension_semantics=sem, vmem_limit_bytes=VMEM_LIMIT)


def _rms(x, g):
    return x * lax.rsqrt(jnp.mean(x * x, axis=-1, keepdims=True) + EPS) * g


def _ffn_cast_body(x_ref, gpre_ref, wg_ref, wu_ref, wd_ref, gpost_ref, o_ref, wg_o, wu_o, wd_o,
                   h_ref, acc_ref):
    f = pl.program_id(1)

    @pl.when(f == 0)
    def _():
        h_ref[...] = _rms(x_ref[...], gpre_ref[...]).astype(BF16)
        acc_ref[...] = jnp.zeros_like(acc_ref)

    for src, dst in ((wg_ref, wg_o), (wu_ref, wu_o), (wd_ref, wd_o)):
        dst[...] = src[...].astype(BF16)
    h = h_ref[...]
    g = jnp.dot(h, wg_o[...], preferred_element_type=F32)
    u = jnp.dot(h, wu_o[...], preferred_element_type=F32)
    a = ((g * jax.nn.sigmoid(g)) * u).astype(BF16)
    acc_ref[...] += jnp.dot(a, wd_o[...], preferred_element_type=F32)

    @pl.when(f == pl.num_programs(1) - 1)
    def _():
        o_ref[...] = x_ref[...] + FFN_RES * _rms(acc_ref[...], gpost_ref[...])


def _ffn_skew_body(x_ref, gpre_ref, wg_ref, wu_ref, wd_ref, gpost_ref, o_ref,
                   h_ref, acc_ref, a0_ref, a1_ref, *, nf):
    f = pl.program_id(1)

    def gate_up(a_ref):
        h = h_ref[...]
        g = jnp.dot(h, wg_ref[...], preferred_element_type=F32)
        u = jnp.dot(h, wu_ref[...], preferred_element_type=F32)
        a_ref[...] = ((g * jax.nn.sigmoid(g)) * u).astype(BF16)

    def down(a_ref):
        acc_ref[...] += jnp.dot(a_ref[...], wd_ref[...], preferred_element_type=F32)

    @pl.when(f == 0)
    def _():
        h_ref[...] = _rms(x_ref[...], gpre_ref[...]).astype(BF16)
        acc_ref[...] = jnp.zeros_like(acc_ref)
        gate_up(a0_ref)

    @pl.when((f > 0) & (f < nf) & ((f & 1) == 1))
    def _():
        gate_up(a1_ref)
        down(a0_ref)

    @pl.when((f > 0) & (f < nf) & ((f & 1) == 0))
    def _():
        gate_up(a0_ref)
        down(a1_ref)

    @pl.when(f == nf)
    def _():
        down(a0_ref if (nf - 1) % 2 == 0 else a1_ref)
        o_ref[...] = x_ref[...] + FFN_RES * _rms(acc_ref[...], gpost_ref[...])


def _ffn(x, gpre, wg, wu, wd, gpost):
    m = x.shape[0]
    cast = wg.dtype == F32
    tm = min(m, 512)
    tf = 256 if cast else 512
    nf = D_FF // tf
    row = lambda i, f: (i, 0)
    const = lambda i, f: (0, 0)
    if cast:
        assert m == tm, "the casting variant writes each weight tile once"
        body = _ffn_cast_body
        steps = nf
        up_map = lambda i, f: (0, f)
        down_map = lambda i, f: (f, 0)
        scratch = []
    else:
        body = functools.partial(_ffn_skew_body, nf=nf)
        steps = nf + 1
        up_map = lambda i, f: (0, jnp.minimum(f, nf - 1))
        down_map = lambda i, f: (jnp.maximum(f - 1, 0), 0)
        scratch = [pltpu.VMEM((tm, tf), BF16), pltpu.VMEM((tm, tf), BF16)]
    w_specs = [
        pl.BlockSpec((D_MODEL, tf), up_map),
        pl.BlockSpec((D_MODEL, tf), up_map),
        pl.BlockSpec((tf, D_MODEL), down_map),
    ]
    out_specs = [pl.BlockSpec((tm, D_MODEL), row)]
    out_shape = [jax.ShapeDtypeStruct((m, D_MODEL), F32)]
    if cast:
        out_specs += w_specs
        out_shape += [jax.ShapeDtypeStruct(w.shape, BF16) for w in (wg, wu, wd)]
    return pl.pallas_call(
        body,
        grid=(m // tm, steps),
        in_specs=[pl.BlockSpec((tm, D_MODEL), row), pl.BlockSpec((1, D_MODEL), const)] + w_specs
        + [pl.BlockSpec((1, D_MODEL), const)],
        out_specs=out_specs,
        out_shape=out_shape,
        scratch_shapes=[pltpu.VMEM((tm, D_MODEL), BF16), pltpu.VMEM((tm, D_MODEL), F32)] + scratch,
        compiler_params=_params(("parallel", "arbitrary")),
        name="ffn_cast" if cast else "ffn",
    )(x, gpre, wg, wu, wd, gpost)


def _proj_in_body(*refs, cast):
    if cast:
        x_ref, g_ref, w_ref, wgate_ref, proj_ref, gates_ref, w_o, h_ref = refs
    else:
        x_ref, g_ref, w_ref, wgate_ref, proj_ref, gates_ref, h_ref = refs
    nt = (((1,), (1,)), ((), ()))

    @pl.when(pl.program_id(1) == 0)
    def _():
        h = _rms(x_ref[...], g_ref[...]).astype(BF16)
        h_ref[...] = h
        gates_ref[...] = lax.dot_general(h, wgate_ref[...], nt, preferred_element_type=F32)

    if cast:
        w_o[...] = w_ref[0].astype(BF16)
        w_ref = w_o
    proj_ref[...] = lax.dot_general(h_ref[...], w_ref[...], nt, preferred_element_type=F32)


def _proj_in(x, g, w_main_t, w_gate_t, *, tn=1024):
    m = x.shape[0]
    cast = w_main_t.dtype == F32
    tm = min(m, 1024)
    if cast:
        assert m == tm, "the casting variant writes each weight tile once"
        w_spec = pl.BlockSpec((1, tn, D_MODEL), lambda i, n: (0, n, 0))
    else:
        w_spec = pl.BlockSpec((tn, D_MODEL), lambda i, n: (n, 0))
    out_specs = [pl.BlockSpec((tm, tn), lambda i, n: (i, n)), pl.BlockSpec((tm, LANES), lambda i, n: (i, 0))]
    out_shape = [jax.ShapeDtypeStruct((m, D_MAIN), F32), jax.ShapeDtypeStruct((m, LANES), F32)]
    if cast:
        out_specs.append(pl.BlockSpec((tn, D_MODEL), lambda i, n: (n, 0)))
        out_shape.append(jax.ShapeDtypeStruct((D_MAIN, D_MODEL), BF16))
    return pl.pallas_call(
        functools.partial(_proj_in_body, cast=cast),
        grid=(m // tm, D_MAIN // tn),
        in_specs=[
            pl.BlockSpec((tm, D_MODEL), lambda i, n: (i, 0)),
            pl.BlockSpec((1, D_MODEL), lambda i, n: (0, 0)),
            w_spec,
            pl.BlockSpec((LANES, D_MODEL), lambda i, n: (0, 0)),
        ],
        out_specs=out_specs,
        out_shape=out_shape,
        scratch_shapes=[pltpu.VMEM((tm, D_MODEL), BF16)],
        compiler_params=_params(("parallel", "arbitrary")),
        name="proj_in_cast" if cast else "proj_in",
    )(x, g, w_main_t, w_gate_t)


def _log_sigmoid(x):
    return jnp.minimum(x, 0.0) - jnp.log1p(jnp.exp(-jnp.abs(x)))


def _seg_cumsum(x, seg_len):
    pos = lax.broadcasted_iota(jnp.int32, x.shape, 0) & (seg_len - 1)
    shift = 1
    while shift < seg_len:
        x = x + jnp.where(pos >= shift, pltpu.roll(x, shift, 0), 0.0)
        shift *= 2
    return x


def _gate_terms(pre, seg_len):
    bt = _seg_cumsum(_log_sigmoid(pre), seg_len)
    return pre, bt, pre.T, bt.T


def _tile_masks(seg_len):
    log2 = seg_len.bit_length() - 1
    t_idx = lax.broadcasted_iota(jnp.int32, (ROWS, ROWS), 0)
    s_idx = lax.broadcasted_iota(jnp.int32, (ROWS, ROWS), 1)
    if seg_len == ROWS:
        return s_idx <= t_idx, None, None
    same = (t_idx >> log2) == (s_idx >> log2)
    last = s_idx == ((t_idx >> log2) << log2) + (seg_len - 1)
    return same & (s_idx <= t_idx), same, last


def _pick(x, j, axis):
    if isinstance(j, int):
        return x[:, j:j + 1] if axis == 1 else x[j:j + 1, :]
    idx = lax.broadcasted_iota(jnp.int32, x.shape, axis)
    return jnp.sum(jnp.where(idx == j, x, 0.0), axis=axis, keepdims=True)


def _mlstm_tile(q, k, v, gates, masks, hd, seg_len, m_prev):
    pre, bt_all, pre_t, bt_t = gates
    valid, same, last = masks
    ig_col = _pick(pre, hd, 1)
    bt_col = _pick(bt_all, hd + N_HEADS, 1)
    key_w = _pick(pre_t, hd, 0) - _pick(bt_t, hd + N_HEADS, 0)

    d = jnp.where(valid, bt_col + key_w, -jnp.inf)
    inter = bt_col + m_prev
    m_t = jnp.maximum(inter, jnp.max(d, axis=1, keepdims=True))
    w_intra = jnp.exp(d - m_t)
    w_inter = jnp.exp(inter - m_t)

    if seg_len == ROWS:
        bt_last = bt_col[ROWS - 1:ROWS, :]
        m_end = m_t[ROWS - 1:ROWS, :]
    else:
        bt_row = _pick(bt_t, hd + N_HEADS, 0)
        bt_last = jnp.sum(jnp.where(last, bt_row, 0.0), axis=1, keepdims=True)
        e = jnp.where(same, bt_last + key_w, -jnp.inf)
        m_end = jnp.maximum(bt_last + m_prev, jnp.max(e, axis=1, keepdims=True))
    g_keys = jnp.exp(bt_last - bt_col + ig_col - m_end)
    g_state = jnp.exp(bt_last + m_prev - m_end)

    ks = k * K_SCALE
    q_bf = q.astype(BF16)
    v_bf = v.astype(BF16)
    s = lax.dot_general(q_bf, ks.astype(BF16), (((1,), (1,)), ((), ())),
                        preferred_element_type=F32) * w_intra
    num = jnp.dot(s.astype(BF16), v_bf, preferred_element_type=F32)
    den = jnp.sum(s, axis=1, keepdims=True)
    kg = ks * g_keys
    return dict(q_bf=q_bf, v_bf=v_bf, num=num, den=den, kg=kg, m_t=m_t, w_inter=w_inter,
                g_state=g_state, m_end=m_end)


def _mlstm_out(t, q, o, q_c, q_n):
    num = t["num"] + q_c * t["w_inter"]
    den = t["den"] + q_n * t["w_inter"]
    h = num / jnp.maximum(jnp.abs(den), jnp.exp(-t["m_t"]))
    return (jax.nn.sigmoid(o) * h).astype(BF16)


def _mlstm_prompt_body(q_ref, k_ref, v_ref, o_ref, gates_ref, bias_ref,
                       hm_ref, c_out_ref, n_out_ref, m_out_ref, c_ref, n_ref, m_ref, *, n_chunks):
    tt = pl.program_id(1)

    @pl.when(tt == 0)
    def _():
        c_ref[...] = jnp.zeros_like(c_ref)
        n_ref[...] = jnp.zeros_like(n_ref)
        m_ref[...] = jnp.zeros_like(m_ref)

    masks = _tile_masks(ROWS)
    for c in range(n_chunks):
        rows = pl.ds(c * ROWS, ROWS)
        gates = _gate_terms(gates_ref[rows, :] + bias_ref[...], ROWS)
        for hd in range(N_HEADS):
            cols = pl.ds(hd * HEAD_DIM, HEAD_DIM)
            q = q_ref[rows, cols]
            t = _mlstm_tile(q, k_ref[rows, cols], v_ref[rows, cols], gates, masks, hd, ROWS,
                            m_ref[hd][:, 0:1])
            c_old = c_ref[hd]
            n_old = n_ref[hd]
            q_c = jnp.dot(t["q_bf"], c_old.astype(BF16), preferred_element_type=F32)
            q_n = jnp.sum(q * n_old, axis=1, keepdims=True)
            hm_ref[rows, cols] = _mlstm_out(t, q, o_ref[rows, cols], q_c, q_n)
            g = t["g_state"]
            c_ref[hd] = g * c_old + lax.dot_general(
                t["kg"].astype(BF16), t["v_bf"], (((0,), (0,)), ((), ())), preferred_element_type=F32)
            n_ref[hd] = g * n_old + jnp.sum(t["kg"], axis=0, keepdims=True)
            m_ref[hd] = jnp.broadcast_to(t["m_end"], (1, LANES))

    @pl.when(tt == pl.num_programs(1) - 1)
    def _():
        c_out_ref[0] = c_ref[...]
        n_out_ref[0] = n_ref[...]
        m_out_ref[0] = m_ref[...]


def _mlstm_prompt(proj, gates, bias, batch, seq, *, tt=256):
    nt = seq // tt
    col = lambda j: (lambda b, t: (b * nt + t, j))
    state = lambda b, t: (b, 0, 0, 0)
    return pl.pallas_call(
        functools.partial(_mlstm_prompt_body, n_chunks=tt // ROWS),
        grid=(batch, nt),
        in_specs=[
            pl.BlockSpec((tt, D_MLSTM), col(0)),
            pl.BlockSpec((tt, D_MLSTM), col(1)),
            pl.BlockSpec((tt, D_MLSTM), col(2)),
            pl.BlockSpec((tt, D_MLSTM), col(3)),
            pl.BlockSpec((tt, LANES), lambda b, t: (b * nt + t, 0)),
            pl.BlockSpec((1, LANES), lambda b, t: (0, 0)),
        ],
        out_specs=[
            pl.BlockSpec((tt, D_MLSTM), lambda b, t: (b * nt + t, 0)),
            pl.BlockSpec((1, N_HEADS, HEAD_DIM, HEAD_DIM), state),
            pl.BlockSpec((1, N_HEADS, 1, HEAD_DIM), state),
            pl.BlockSpec((1, N_HEADS, 1, LANES), state),
        ],
        out_shape=[
            jax.ShapeDtypeStruct((batch * seq, D_MLSTM), BF16),
            jax.ShapeDtypeStruct((batch, N_HEADS, HEAD_DIM, HEAD_DIM), F32),
            jax.ShapeDtypeStruct((batch, N_HEADS, 1, HEAD_DIM), F32),
            jax.ShapeDtypeStruct((batch, N_HEADS, 1, LANES), F32),
        ],
        scratch_shapes=[pltpu.VMEM((N_HEADS, HEAD_DIM, HEAD_DIM), F32), pltpu.VMEM((N_HEADS, 1, HEAD_DIM), F32),
                        pltpu.VMEM((N_HEADS, 1, LANES), F32)],
        compiler_params=_params(("parallel", "arbitrary")),
        name="mlstm_prompt",
    )(proj, proj, proj, proj, gates, bias)


def _mlstm_sample_body(q_ref, k_ref, v_ref, o_ref, gates_ref, bias_ref, mrow_ref, c_ref, n_ref,
                       hm_ref, c_out_ref, n_out_ref, m_out_ref, *, seg_len):
    hd = pl.program_id(1)
    n_seg = ROWS // seg_len
    grp = 16 // seg_len
    log2 = seg_len.bit_length() - 1
    q = q_ref[...]
    gates = _gate_terms(gates_ref[...] + bias_ref[...], seg_len)
    t = _mlstm_tile(q, k_ref[...], v_ref[...], gates, _tile_masks(seg_len), hd, seg_len, mrow_ref[0])

    seg_of_row = lax.broadcasted_iota(jnp.int32, (16, 1), 0) >> log2
    qc_parts, n_parts = [], []
    for j in range(ROWS // 16):
        qg = t["q_bf"][16 * j:16 * (j + 1)]
        qc, nr = None, None
        for i in range(grp):
            b = grp * j + i
            r = jnp.dot(qg, c_ref[0, b, 0].astype(BF16), preferred_element_type=F32)
            nb = jnp.broadcast_to(n_ref[b, 0], (16, HEAD_DIM))
            qc = r if i == 0 else jnp.where(seg_of_row == i, r, qc)
            nr = nb if i == 0 else jnp.where(seg_of_row == i, nb, nr)
        qc_parts.append(qc)
        n_parts.append(nr)
    q_c = jnp.concatenate(qc_parts, axis=0)
    q_n = jnp.sum(q * jnp.concatenate(n_parts, axis=0), axis=1, keepdims=True)
    hm_ref[...] = _mlstm_out(t, q, o_ref[...], q_c, q_n)

    kg = t["kg"]
    kg_t = kg.T
    seg_of_lane = lax.broadcasted_iota(jnp.int32, (1, ROWS), 1) >> log2
    seg_of_row8 = lax.broadcasted_iota(jnp.int32, (8, 1), 0) >> log2
    per8 = 8 // seg_len
    for b in range(n_seg):
        g = t["g_state"][seg_len * b:seg_len * b + 1, :]
        upd = jnp.dot(jnp.where(seg_of_lane == b, kg_t, 0.0).astype(BF16), t["v_bf"],
                      preferred_element_type=F32)
        c_out_ref[0, b, 0] = g * c_ref[0, b, 0] + upd
        kg8 = kg[8 * (b // per8):8 * (b // per8) + 8]
        n_out_ref[b, 0] = g * n_ref[b, 0] + jnp.sum(
            jnp.where(seg_of_row8 == (b % per8), kg8, 0.0), axis=0, keepdims=True)
    m_out_ref[0] = t["m_end"]


def _mlstm_sample(proj, gates, bias, m_rows, c0, n0, seg_len):
    m = proj.shape[0]
    n_seg = ROWS // seg_len
    col = lambda j: (lambda i, h: (i, j * N_HEADS + h))
    return pl.pallas_call(
        functools.partial(_mlstm_sample_body, seg_len=seg_len),
        grid=(m // ROWS, N_HEADS),
        in_specs=[
            pl.BlockSpec((ROWS, HEAD_DIM), col(0)),
            pl.BlockSpec((ROWS, HEAD_DIM), col(1)),
            pl.BlockSpec((ROWS, HEAD_DIM), col(2)),
            pl.BlockSpec((ROWS, HEAD_DIM), col(3)),
            pl.BlockSpec((ROWS, LANES), lambda i, h: (i, 0)),
            pl.BlockSpec((1, LANES), lambda i, h: (0, 0)),
            pl.BlockSpec((1, ROWS, 1), lambda i, h: (h, i, 0)),
            pl.BlockSpec((1, n_seg, 1, HEAD_DIM, HEAD_DIM), lambda i, h: (0, i, h, 0, 0)),
            pl.BlockSpec((n_seg, 1, 1, HEAD_DIM), lambda i, h: (i, h, 0, 0)),
        ],
        out_specs=[
            pl.BlockSpec((ROWS, HEAD_DIM), lambda i, h: (i, h)),
            pl.BlockSpec((1, n_seg, 1, HEAD_DIM, HEAD_DIM), lambda i, h: (0, i, h, 0, 0)),
            pl.BlockSpec((n_seg, 1, 1, HEAD_DIM), lambda i, h: (i, h, 0, 0)),
            pl.BlockSpec((1, ROWS, 1), lambda i, h: (h, i, 0)),
        ],
        out_shape=[
            jax.ShapeDtypeStruct((m, D_MLSTM), BF16),
            jax.ShapeDtypeStruct(c0.shape, F32),
            jax.ShapeDtypeStruct(n0.shape, F32),
            jax.ShapeDtypeStruct(m_rows.shape, F32),
        ],
        compiler_params=_params(("parallel", "parallel")),
        name="mlstm_sample",
    )(proj, proj, proj, proj, gates, bias, m_rows, c0, n0)


def _ln_swish(y, g, b):
    mu = jnp.mean(y, axis=-1, keepdims=True)
    yc = y - mu
    var = jnp.mean(yc * yc, axis=-1, keepdims=True)
    z = yc * lax.rsqrt(var + EPS) * g + b
    return z * jax.nn.sigmoid(z)


def _conv_prompt_body(a_ref, b_ref, w_ref, bdw_ref, gln_ref, bln_ref, hc_ref, st_ref,
                      u_ref, us_ref, wb_ref, y_ref, *, tt):
    t_id = pl.program_id(1)
    n_shift = tt + CONV_PAD - 8

    @pl.when(t_id == 0)
    def _():
        u_ref[0:CONV_PAD, :] = jnp.zeros((CONV_PAD, D_CONV), F32)
        for s in range(CONV_WIDTH):
            wb_ref[s] = jnp.broadcast_to(w_ref[s:s + 1, :], (8, D_CONV))

    u_ref[CONV_PAD:CONV_PAD + tt, :] = a_ref[...] * jax.nn.sigmoid(b_ref[...])
    for r in range(1, 8):
        us_ref[r - 1] = u_ref[r:r + n_shift, :]

    def row_block(i, carry):
        base = pl.multiple_of(i * CONV_ROWS, CONV_ROWS)
        n_slab = CONV_ROWS // 8
        acc = [jnp.broadcast_to(bdw_ref[...], (8, D_CONV))] * n_slab
        for s in range(CONV_WIDTH):
            k8, r = divmod(CONV_PAD - HALO + s, 8)
            w = wb_ref[s]
            for j in range(n_slab):
                rows = pl.ds(pl.multiple_of(base + 8 * (k8 + j), 8), 8)
                win = u_ref[rows, :] if r == 0 else us_ref[r - 1, rows, :]
                acc[j] = acc[j] + w * win
        for j in range(n_slab):
            y_ref[pl.ds(pl.multiple_of(base + 8 * j, 8), 8), :] = acc[j]
        return carry

    lax.fori_loop(0, tt // CONV_ROWS, row_block, 0)
    hc_ref[...] = _ln_swish(y_ref[...], gln_ref[...], bln_ref[...]).astype(BF16)

    @pl.when(t_id == pl.num_programs(1) - 1)
    def _():
        st_ref[0, 0] = u_ref[CONV_PAD + tt - HALO:CONV_PAD + tt, :]

    u_ref[0:CONV_PAD, :] = u_ref[tt:tt + CONV_PAD, :]


def _conv_prompt(proj, w_dw, b_dw, g_ln, b_ln, batch, seq, *, tt=256):
    nt = seq // tt
    const = lambda b, t: (0, 0)
    return pl.pallas_call(
        functools.partial(_conv_prompt_body, tt=tt),
        grid=(batch, nt),
        in_specs=[
            pl.BlockSpec((tt, D_CONV), lambda b, t: (b * nt + t, 4)),
            pl.BlockSpec((tt, D_CONV), lambda b, t: (b * nt + t, 5)),
            pl.BlockSpec((CONV_WIDTH, D_CONV), const),
            pl.BlockSpec((1, D_CONV), const),
            pl.BlockSpec((1, D_CONV), const),
            pl.BlockSpec((1, D_CONV), const),
        ],
        out_specs=[
            pl.BlockSpec((tt, D_CONV), lambda b, t: (b * nt + t, 0)),
            pl.BlockSpec((1, 1, HALO, D_CONV), lambda b, t: (0, b, 0, 0)),
        ],
        out_shape=[
            jax.ShapeDtypeStruct((batch * seq, D_CONV), BF16),
            jax.ShapeDtypeStruct((1, batch, HALO, D_CONV), F32),
        ],
        scratch_shapes=[pltpu.VMEM((tt + CONV_PAD, D_CONV), F32),
                        pltpu.VMEM((7, tt + CONV_PAD - 8, D_CONV), F32),
                        pltpu.VMEM((CONV_WIDTH, 8, D_CONV), F32),
                        pltpu.VMEM((tt, D_CONV), F32)],
        compiler_params=_params(("parallel", "arbitrary")),
        name="conv_prompt",
    )(proj, proj, w_dw, b_dw, g_ln, b_ln)


def _conv_sample_body(a_ref, b_ref, st_ref, w_ref, bdw_ref, gln_ref, bln_ref, hc_ref, st_out_ref,
                      wb_ref, *, seq):
    @pl.when(pl.program_id(0) == 0)
    def _():
        for s in range(CONV_WIDTH):
            wb_ref[s] = jnp.broadcast_to(w_ref[s:s + 1, :], (8, D_CONV))

    acc = [jnp.broadcast_to(bdw_ref[...], (8, D_CONV))] * seq
    for j in range(HALO + seq):
        slab = st_ref[0, j] if j < HALO else a_ref[j - HALO] * jax.nn.sigmoid(b_ref[j - HALO])
        for t in range(seq):
            if 0 <= j - t < CONV_WIDTH:
                acc[t] = acc[t] + wb_ref[j - t] * slab
        if j >= seq:
            st_out_ref[0, j - seq] = slab
    for t in range(seq):
        hc_ref[t] = _ln_swish(acc[t], gln_ref[...], bln_ref[...])


def _conv_sample(a_t, b_t, state_t, w_dw, b_dw, g_ln, b_ln):
    seq, batch, _ = a_t.shape
    bb = 8
    const = lambda i: (0, 0)
    tok = pl.BlockSpec((seq, bb, D_CONV), lambda i: (0, i, 0))
    hist = pl.BlockSpec((1, HALO, bb, D_CONV), lambda i: (0, 0, i, 0))
    return pl.pallas_call(
        functools.partial(_conv_sample_body, seq=seq),
        grid=(batch // bb,),
        in_specs=[tok, tok, hist,
                  pl.BlockSpec((CONV_WIDTH, D_CONV), const),
                  pl.BlockSpec((1, D_CONV), const),
                  pl.BlockSpec((1, D_CONV), const),
                  pl.BlockSpec((1, D_CONV), const)],
        out_specs=[tok, hist],
        out_shape=[
            jax.ShapeDtypeStruct(a_t.shape, F32),
            jax.ShapeDtypeStruct(state_t.shape, F32),
        ],
        scratch_shapes=[pltpu.VMEM((CONV_WIDTH, 8, D_CONV), F32)],
        compiler_params=_params(("arbitrary",)),
        name="conv_sample",
    )(a_t, b_t, state_t, w_dw, b_dw, g_ln, b_ln)


def _proj_out_body(hm_ref, hc_ref, x_ref, wa_ref, wb_ref, g_ref, o_ref):
    mix = (jnp.dot(hm_ref[...], wa_ref[...], preferred_element_type=F32)
           + jnp.dot(hc_ref[...].astype(BF16), wb_ref[...], preferred_element_type=F32))
    o_ref[...] = x_ref[...] + _rms(mix, g_ref[...])


def _proj_out(hm, hc, x, w_out, g, *, tm=512):
    m = x.shape[0]
    row = lambda i: (i, 0)
    return pl.pallas_call(
        _proj_out_body,
        grid=(m // tm,),
        in_specs=[
            pl.BlockSpec((tm, D_MLSTM), row),
            pl.BlockSpec((tm, D_CONV), row),
            pl.BlockSpec((tm, D_MODEL), row),
            pl.BlockSpec((D_MLSTM, D_MODEL), lambda i: (0, 0)),
            pl.BlockSpec((D_CONV, D_MODEL), lambda i: (1, 0)),
            pl.BlockSpec((1, D_MODEL), lambda i: (0, 0)),
        ],
        out_specs=pl.BlockSpec((tm, D_MODEL), row),
        out_shape=jax.ShapeDtypeStruct((m, D_MODEL), F32),
        compiler_params=_params(("parallel",)),
        name="proj_out",
    )(hm, hc, x, w_out, w_out, g)


def kernel(x_prompt, x_sample, state_mlstm_C, state_mlstm_n, state_mlstm_m, state_conv, g_ffn1_pre, w_ffn1_gate, w_ffn1_up, w_ffn1_down, g_ffn1_post, g_mix_pre, w_in, b_igate, b_fgate, w_dw, b_dw, g_conv_ln, b_conv_ln, w_out, g_mix_post, g_ffn2_pre, w_ffn2_gate, w_ffn2_up, w_ffn2_down, g_ffn2_post):
    depth = state_mlstm_C.shape[0]
    assert depth == 1, "kernel handles a single layer"
    bp, tp, _ = x_prompt.shape
    bs, ts, _ = x_sample.shape
    l = 0

    w_in_t = jnp.swapaxes(w_in, 1, 2)
    w_gate_t = jnp.pad(w_in_t[l, D_MAIN:], ((0, LANES - 2 * N_HEADS), (0, 0))).astype(BF16)
    w_o = w_out[l].astype(BF16)
    bias = jnp.pad(jnp.concatenate([b_igate[l], b_fgate[l]]), (0, LANES - 2 * N_HEADS))[None, :]

    xs, *ffn1 = _ffn(x_sample.reshape(bs * ts, D_MODEL), g_ffn1_pre,
                     w_ffn1_gate[l], w_ffn1_up[l], w_ffn1_down[l], g_ffn1_post)
    proj_s, gates_s, w_main_t = _proj_in(xs, g_mix_pre, w_in_t, w_gate_t)
    m_rows = jnp.repeat(state_mlstm_m[l].T, ts, axis=1)[:, :, None]
    hm_s, c_s, n_s, m_s = _mlstm_sample(
        proj_s, gates_s, bias, m_rows, state_mlstm_C, state_mlstm_n[l][:, :, None, :], ts)
    glu_t = jnp.swapaxes(proj_s[:, 4 * D_MLSTM:].reshape(bs, ts, 2, D_CONV), 0, 1)
    hc_t, conv_t = _conv_sample(glu_t[:, :, 0], glu_t[:, :, 1], jnp.swapaxes(state_conv, 1, 2),
                                w_dw[l], b_dw, g_conv_ln, b_conv_ln)
    hc_s = jnp.swapaxes(hc_t, 0, 1).reshape(bs * ts, D_CONV)
    xs = _proj_out(hm_s, hc_s, xs, w_o, g_mix_post)
    ys, *ffn2 = _ffn(xs, g_ffn2_pre, w_ffn2_gate[l], w_ffn2_up[l], w_ffn2_down[l], g_ffn2_post)
    ys = ys.reshape(bs, ts, D_MODEL)

    xp = _ffn(x_prompt.reshape(bp * tp, D_MODEL), g_ffn1_pre, *ffn1, g_ffn1_post)[0]
    proj_p, gates_p = _proj_in(xp, g_mix_pre, w_main_t, w_gate_t)
    hm_p, c_p, n_p, m_p = _mlstm_prompt(proj_p, gates_p, bias, bp, tp)
    hc_p, conv_p = _conv_prompt(proj_p, w_dw[l], b_dw, g_conv_ln, b_conv_ln, bp, tp)
    xp = _proj_out(hm_p, hc_p, xp, w_o, g_mix_post)
    yp = _ffn(xp, g_ffn2_pre, *ffn2, g_ffn2_post)[0].reshape(bp, tp, D_MODEL)

    return (yp, ys,
            c_p[None], n_p[:, :, 0, :][None], m_p[:, :, 0, 0][None], conv_p,
            c_s, n_s[:, :, 0, :][None], m_s[:, ::ts, 0].T[None], jnp.swapaxes(conv_t, 1, 2))
```

```python
import functools

import jax
import jax.numpy as jnp
from jax import lax
from jax.experimental import pallas as pl
from jax.experimental.pallas import tpu as pltpu

D_MODEL = 2048
N_HEADS = 4
HEAD_DIM = 256
D_MLSTM = N_HEADS * HEAD_DIM
D_CONV = D_MODEL - D_MLSTM
CONV_WIDTH = 31
HALO = CONV_WIDTH - 1
CONV_PAD = 32
CONV_ROWS = 32
D_FF = 5632
FFN_TF = 512
D_MAIN = 4 * D_MLSTM + 2 * D_CONV
EPS = 1e-6
FFN_RES = 0.5
K_SCALE = HEAD_DIM ** -0.5

LANES = 128
ROWS = 128
VMEM_LIMIT = 56 * 1024 * 1024

F32 = jnp.float32
BF16 = jnp.bfloat16


def _params(sem):
    return pltpu.CompilerParams(dimension_semantics=sem, vmem_limit_bytes=VMEM_LIMIT)


def _rms(x, g):
    return x * lax.rsqrt(jnp.mean(x * x, axis=-1, keepdims=True) + EPS) * g


def _ffn_cast_body(x_ref, gpre_ref, wg_ref, wu_ref, wd_ref, gpost_ref, o_ref, wg_o, wu_o, wd_o,
                   h_ref, acc_ref):
    f = pl.program_id(1)

    @pl.when(f == 0)
    def _():
        h_ref[...] = _rms(x_ref[...], gpre_ref[...]).astype(BF16)
        acc_ref[...] = jnp.zeros_like(acc_ref)

    wg_o[0] = wg_ref[...].astype(BF16)
    wu_o[0] = wu_ref[...].astype(BF16)
    wd_o[...] = wd_ref[...].astype(BF16)
    h = h_ref[...]
    g = jnp.dot(h, wg_o[0], preferred_element_type=F32)
    u = jnp.dot(h, wu_o[0], preferred_element_type=F32)
    a = ((g * jax.nn.sigmoid(g)) * u).astype(BF16)
    acc_ref[...] += jnp.dot(a, wd_o[...], preferred_element_type=F32)

    @pl.when(f == pl.num_programs(1) - 1)
    def _():
        o_ref[...] = x_ref[...] + FFN_RES * _rms(acc_ref[...], gpost_ref[...])


def _ffn_skew_body(x_ref, gpre_ref, wg_ref, wu_ref, wd_ref, gpost_ref, o_ref,
                   h_ref, acc_ref, a0_ref, a1_ref, *, nf):
    f = pl.program_id(1)

    def gate_up(a_ref):
        h = h_ref[...]
        g = jnp.dot(h, wg_ref[0], preferred_element_type=F32)
        u = jnp.dot(h, wu_ref[0], preferred_element_type=F32)
        a_ref[...] = ((g * jax.nn.sigmoid(g)) * u).astype(BF16)

    def down(a_ref):
        acc_ref[...] += jnp.dot(a_ref[...], wd_ref[...], preferred_element_type=F32)

    @pl.when(f == 0)
    def _():
        h_ref[...] = _rms(x_ref[...], gpre_ref[...]).astype(BF16)
        acc_ref[...] = jnp.zeros_like(acc_ref)
        gate_up(a0_ref)

    @pl.when((f > 0) & (f < nf) & ((f & 1) == 1))
    def _():
        gate_up(a1_ref)
        down(a0_ref)

    @pl.when((f > 0) & (f < nf) & ((f & 1) == 0))
    def _():
        gate_up(a0_ref)
        down(a1_ref)

    @pl.when(f == nf)
    def _():
        down(a0_ref if (nf - 1) % 2 == 0 else a1_ref)
        o_ref[...] = x_ref[...] + FFN_RES * _rms(acc_ref[...], gpost_ref[...])


def _ffn(x, gpre, wg, wu, wd, gpost):
    m = x.shape[0]
    cast = wg.dtype == F32
    tm = min(m, 512)
    nf = D_FF // FFN_TF
    row = lambda i, f: (i, 0)
    const = lambda i, f: (0, 0)
    out_specs = [pl.BlockSpec((tm, D_MODEL), row)]
    out_shape = [jax.ShapeDtypeStruct((m, D_MODEL), F32)]
    if cast:
        assert m == tm, "the casting variant writes each weight tile once"
        tf = FFN_TF // 2
        body = _ffn_cast_body
        steps = D_FF // tf
        up_spec = pl.BlockSpec((D_MODEL, tf), lambda i, f: (0, f))
        down_spec = pl.BlockSpec((tf, D_MODEL), lambda i, f: (f, 0))
        up_out = pl.BlockSpec((1, D_MODEL, tf), lambda i, f: (f // 2, 0, f % 2))
        out_specs += [up_out, up_out, down_spec]
        chunked = jax.ShapeDtypeStruct((nf, D_MODEL, FFN_TF), BF16)
        out_shape += [chunked, chunked, jax.ShapeDtypeStruct(wd.shape, BF16)]
        scratch = []
    else:
        tf = FFN_TF
        body = functools.partial(_ffn_skew_body, nf=nf)
        steps = nf + 1
        up_spec = pl.BlockSpec((1, D_MODEL, tf), lambda i, f: (jnp.minimum(f, nf - 1), 0, 0))
        down_spec = pl.BlockSpec((tf, D_MODEL), lambda i, f: (jnp.maximum(f - 1, 0), 0))
        scratch = [pltpu.VMEM((tm, tf), BF16), pltpu.VMEM((tm, tf), BF16)]
    return pl.pallas_call(
        body,
        grid=(m // tm, steps),
        in_specs=[pl.BlockSpec((tm, D_MODEL), row), pl.BlockSpec((1, D_MODEL), const),
                  up_spec, up_spec, down_spec, pl.BlockSpec((1, D_MODEL), const)],
        out_specs=out_specs,
        out_shape=out_shape,
        scratch_shapes=[pltpu.VMEM((tm, D_MODEL), BF16), pltpu.VMEM((tm, D_MODEL), F32)] + scratch,
        compiler_params=_params(("parallel", "arbitrary")),
        name="ffn_cast" if cast else "ffn",
    )(x, gpre, wg, wu, wd, gpost)


def _proj_in_body(*refs, cast):
    if cast:
        x_ref, g_ref, w_ref, wgate_ref, proj_ref, gates_ref, w_o, h_ref = refs
    else:
        x_ref, g_ref, w_ref, wgate_ref, proj_ref, gates_ref, h_ref = refs
    nt = (((1,), (1,)), ((), ()))

    @pl.when(pl.program_id(1) == 0)
    def _():
        h = _rms(x_ref[...], g_ref[...]).astype(BF16)
        h_ref[...] = h
        gates_ref[...] = lax.dot_general(h, wgate_ref[...], nt, preferred_element_type=F32)

    if cast:
        w_o[...] = w_ref[0].astype(BF16)
        w_ref = w_o
    proj_ref[...] = lax.dot_general(h_ref[...], w_ref[...], nt, preferred_element_type=F32)


def _proj_in(x, g, w_main_t, w_gate_t):
    m = x.shape[0]
    cast = w_main_t.dtype == F32
    tm = min(m, 1024)
    tn = 1024 if cast else 1536
    if cast:
        assert m == tm, "the casting variant writes each weight tile once"
        w_spec = pl.BlockSpec((1, tn, D_MODEL), lambda i, n: (0, n, 0))
    else:
        w_spec = pl.BlockSpec((tn, D_MODEL), lambda i, n: (n, 0))
    out_specs = [pl.BlockSpec((tm, tn), lambda i, n: (i, n)), pl.BlockSpec((tm, LANES), lambda i, n: (i, 0))]
    out_shape = [jax.ShapeDtypeStruct((m, D_MAIN), F32), jax.ShapeDtypeStruct((m, LANES), F32)]
    if cast:
        out_specs.append(pl.BlockSpec((tn, D_MODEL), lambda i, n: (n, 0)))
        out_shape.append(jax.ShapeDtypeStruct((D_MAIN, D_MODEL), BF16))
    return pl.pallas_call(
        functools.partial(_proj_in_body, cast=cast),
        grid=(m // tm, D_MAIN // tn),
        in_specs=[
            pl.BlockSpec((tm, D_MODEL), lambda i, n: (i, 0)),
            pl.BlockSpec((1, D_MODEL), lambda i, n: (0, 0)),
            w_spec,
            pl.BlockSpec((LANES, D_MODEL), lambda i, n: (0, 0)),
        ],
        out_specs=out_specs,
        out_shape=out_shape,
        scratch_shapes=[pltpu.VMEM((tm, D_MODEL), BF16)],
        compiler_params=_params(("parallel", "arbitrary")),
        name="proj_in_cast" if cast else "proj_in",
    )(x, g, w_main_t, w_gate_t)


def _log_sigmoid(x):
    return jnp.minimum(x, 0.0) - jnp.log1p(jnp.exp(-jnp.abs(x)))


def _seg_cumsum(x, seg_len):
    pos = lax.broadcasted_iota(jnp.int32, x.shape, 0) & (seg_len - 1)
    shift = 1
    while shift < seg_len:
        x = x + jnp.where(pos >= shift, pltpu.roll(x, shift, 0), 0.0)
        shift *= 2
    return x


def _gate_terms(pre, seg_len):
    bt = _seg_cumsum(_log_sigmoid(pre), seg_len)
    return pre, bt, pre.T, bt.T


def _tile_masks(seg_len):
    log2 = seg_len.bit_length() - 1
    t_idx = lax.broadcasted_iota(jnp.int32, (ROWS, ROWS), 0)
    s_idx = lax.broadcasted_iota(jnp.int32, (ROWS, ROWS), 1)
    if seg_len == ROWS:
        return s_idx <= t_idx, None, None
    same = (t_idx >> log2) == (s_idx >> log2)
    last = s_idx == ((t_idx >> log2) << log2) + (seg_len - 1)
    return same & (s_idx <= t_idx), same, last


def _pick(x, j, axis):
    if isinstance(j, int):
        return x[:, j:j + 1] if axis == 1 else x[j:j + 1, :]
    idx = lax.broadcasted_iota(jnp.int32, x.shape, axis)
    return jnp.sum(jnp.where(idx == j, x, 0.0), axis=axis, keepdims=True)


def _mlstm_tile(q, k, v, gates, masks, hd, seg_len, m_prev):
    pre, bt_all, pre_t, bt_t = gates
    valid, same, last = masks
    ig_col = _pick(pre, hd, 1)
    bt_col = _pick(bt_all, hd + N_HEADS, 1)
    key_w = _pick(pre_t, hd, 0) - _pick(bt_t, hd + N_HEADS, 0)

    d = jnp.where(valid, bt_col + key_w, -jnp.inf)
    inter = bt_col + m_prev
    m_t = jnp.maximum(inter, jnp.max(d, axis=1, keepdims=True))
    w_intra = jnp.exp(d - m_t)
    w_inter = jnp.exp(inter - m_t)

    if seg_len == ROWS:
        bt_last = bt_col[ROWS - 1:ROWS, :]
        m_end = m_t[ROWS - 1:ROWS, :]
    else:
        bt_row = _pick(bt_t, hd + N_HEADS, 0)
        bt_last = jnp.sum(jnp.where(last, bt_row, 0.0), axis=1, keepdims=True)
        e = jnp.where(same, bt_last + key_w, -jnp.inf)
        m_end = jnp.maximum(bt_last + m_prev, jnp.max(e, axis=1, keepdims=True))
    g_keys = jnp.exp(bt_last - bt_col + ig_col - m_end)
    g_state = jnp.exp(bt_last + m_prev - m_end)

    ks = k * K_SCALE
    q_bf = q.astype(BF16)
    v_bf = v.astype(BF16)
    s = lax.dot_general(q_bf, ks.astype(BF16), (((1,), (1,)), ((), ())),
                        preferred_element_type=F32) * w_intra
    num = jnp.dot(s.astype(BF16), v_bf, preferred_element_type=F32)
    den = jnp.sum(s, axis=1, keepdims=True)
    kg = ks * g_keys
    return dict(q_bf=q_bf, v_bf=v_bf, num=num, den=den, kg=kg, m_t=m_t, w_inter=w_inter,
                g_state=g_state, m_end=m_end)


def _mlstm_out(t, q, o, q_c, q_n):
    num = t["num"] + q_c * t["w_inter"]
    den = t["den"] + q_n * t["w_inter"]
    h = num / jnp.maximum(jnp.abs(den), jnp.exp(-t["m_t"]))
    return (jax.nn.sigmoid(o) * h).astype(BF16)


def _mlstm_prompt_body(q_ref, k_ref, v_ref, o_ref, gates_ref, bias_ref,
                       hm_ref, c_out_ref, n_out_ref, m_out_ref, c_ref, n_ref, m_ref, *, n_chunks):
    tt = pl.program_id(1)

    @pl.when(tt == 0)
    def _():
        c_ref[...] = jnp.zeros_like(c_ref)
        n_ref[...] = jnp.zeros_like(n_ref)
        m_ref[...] = jnp.zeros_like(m_ref)

    masks = _tile_masks(ROWS)
    for c in range(n_chunks):
        rows = pl.ds(c * ROWS, ROWS)
        gates = _gate_terms(gates_ref[rows, :] + bias_ref[...], ROWS)
        for hd in range(N_HEADS):
            cols = pl.ds(hd * HEAD_DIM, HEAD_DIM)
            q = q_ref[rows, cols]
            t = _mlstm_tile(q, k_ref[rows, cols], v_ref[rows, cols], gates, masks, hd, ROWS,
                            m_ref[hd][:, 0:1])
            c_old = c_ref[hd]
            n_old = n_ref[hd]
            q_c = jnp.dot(t["q_bf"], c_old.astype(BF16), preferred_element_type=F32)
            q_n = jnp.sum(q * n_old, axis=1, keepdims=True)
            hm_ref[rows, cols] = _mlstm_out(t, q, o_ref[rows, cols], q_c, q_n)
            g = t["g_state"]
            c_ref[hd] = g * c_old + lax.dot_general(
                t["kg"].astype(BF16), t["v_bf"], (((0,), (0,)), ((), ())), preferred_element_type=F32)
            n_ref[hd] = g * n_old + jnp.sum(t["kg"], axis=0, keepdims=True)
            m_ref[hd] = jnp.broadcast_to(t["m_end"], (1, LANES))

    @pl.when(tt == pl.num_programs(1) - 1)
    def _():
        c_out_ref[0] = c_ref[...]
        n_out_ref[0] = n_ref[...]
        m_out_ref[0] = m_ref[...]


def _mlstm_prompt(proj, gates, bias, batch, seq, *, tt=256):
    nt = seq // tt
    col = lambda j: (lambda b, t: (b * nt + t, j))
    state = lambda b, t: (b, 0, 0, 0)
    return pl.pallas_call(
        functools.partial(_mlstm_prompt_body, n_chunks=tt // ROWS),
        grid=(batch, nt),
        in_specs=[
            pl.BlockSpec((tt, D_MLSTM), col(0)),
            pl.BlockSpec((tt, D_MLSTM), col(1)),
            pl.BlockSpec((tt, D_MLSTM), col(2)),
            pl.BlockSpec((tt, D_MLSTM), col(3)),
            pl.BlockSpec((tt, LANES), lambda b, t: (b * nt + t, 0)),
            pl.BlockSpec((1, LANES), lambda b, t: (0, 0)),
        ],
        out_specs=[
            pl.BlockSpec((tt, D_MLSTM), lambda b, t: (b * nt + t, 0)),
            pl.BlockSpec((1, N_HEADS, HEAD_DIM, HEAD_DIM), state),
            pl.BlockSpec((1, N_HEADS, 1, HEAD_DIM), state),
            pl.BlockSpec((1, N_HEADS, 1, LANES), state),
        ],
        out_shape=[
            jax.ShapeDtypeStruct((batch * seq, D_MLSTM), BF16),
            jax.ShapeDtypeStruct((batch, N_HEADS, HEAD_DIM, HEAD_DIM), F32),
            jax.ShapeDtypeStruct((batch, N_HEADS, 1, HEAD_DIM), F32),
            jax.ShapeDtypeStruct((batch, N_HEADS, 1, LANES), F32),
        ],
        scratch_shapes=[pltpu.VMEM((N_HEADS, HEAD_DIM, HEAD_DIM), F32), pltpu.VMEM((N_HEADS, 1, HEAD_DIM), F32),
                        pltpu.VMEM((N_HEADS, 1, LANES), F32)],
        compiler_params=_params(("parallel", "arbitrary")),
        name="mlstm_prompt",
    )(proj, proj, proj, proj, gates, bias)


def _mlstm_sample_body(q_ref, k_ref, v_ref, o_ref, gates_ref, bias_ref, mrow_ref, c_ref, n_ref,
                       hm_ref, c_out_ref, n_out_ref, m_out_ref, *, seg_len):
    hd = pl.program_id(1)
    n_seg = ROWS // seg_len
    grp = 16 // seg_len
    log2 = seg_len.bit_length() - 1
    q = q_ref[...]
    gates = _gate_terms(gates_ref[...] + bias_ref[...], seg_len)
    t = _mlstm_tile(q, k_ref[...], v_ref[...], gates, _tile_masks(seg_len), hd, seg_len, mrow_ref[0])

    seg_of_row = lax.broadcasted_iota(jnp.int32, (16, 1), 0) >> log2
    qc_parts, n_parts = [], []
    for j in range(ROWS // 16):
        qg = t["q_bf"][16 * j:16 * (j + 1)]
        qc, nr = None, None
        for i in range(grp):
            b = grp * j + i
            r = jnp.dot(qg, c_ref[0, b, 0].astype(BF16), preferred_element_type=F32)
            nb = jnp.broadcast_to(n_ref[b, 0], (16, HEAD_DIM))
            qc = r if i == 0 else jnp.where(seg_of_row == i, r, qc)
            nr = nb if i == 0 else jnp.where(seg_of_row == i, nb, nr)
        qc_parts.append(qc)
        n_parts.append(nr)
    q_c = jnp.concatenate(qc_parts, axis=0)
    q_n = jnp.sum(q * jnp.concatenate(n_parts, axis=0), axis=1, keepdims=True)
    hm_ref[...] = _mlstm_out(t, q, o_ref[...], q_c, q_n)

    kg = t["kg"]
    kg_t = kg.T
    seg_of_lane = lax.broadcasted_iota(jnp.int32, (1, ROWS), 1) >> log2
    seg_of_row8 = lax.broadcasted_iota(jnp.int32, (8, 1), 0) >> log2
    per8 = 8 // seg_len
    for b in range(n_seg):
        g = t["g_state"][seg_len * b:seg_len * b + 1, :]
        upd = jnp.dot(jnp.where(seg_of_lane == b, kg_t, 0.0).astype(BF16), t["v_bf"],
                      preferred_element_type=F32)
        c_out_ref[0, b, 0] = g * c_ref[0, b, 0] + upd
        kg8 = kg[8 * (b // per8):8 * (b // per8) + 8]
        n_out_ref[b, 0] = g * n_ref[b, 0] + jnp.sum(
            jnp.where(seg_of_row8 == (b % per8), kg8, 0.0), axis=0, keepdims=True)
    m_out_ref[0] = t["m_end"]


def _mlstm_sample(proj, gates, bias, m_rows, c0, n0, seg_len):
    m = proj.shape[0]
    n_seg = ROWS // seg_len
    col = lambda j: (lambda i, h: (i, j * N_HEADS + h))
    return pl.pallas_call(
        functools.partial(_mlstm_sample_body, seg_len=seg_len),
        grid=(m // ROWS, N_HEADS),
        in_specs=[
            pl.BlockSpec((ROWS, HEAD_DIM), col(0)),
            pl.BlockSpec((ROWS, HEAD_DIM), col(1)),
            pl.BlockSpec((ROWS, HEAD_DIM), col(2)),
            pl.BlockSpec((ROWS, HEAD_DIM), col(3)),
            pl.BlockSpec((ROWS, LANES), lambda i, h: (i, 0)),
            pl.BlockSpec((1, LANES), lambda i, h: (0, 0)),
            pl.BlockSpec((1, ROWS, 1), lambda i, h: (h, i, 0)),
            pl.BlockSpec((1, n_seg, 1, HEAD_DIM, HEAD_DIM), lambda i, h: (0, i, h, 0, 0)),
            pl.BlockSpec((n_seg, 1, 1, HEAD_DIM), lambda i, h: (i, h, 0, 0)),
        ],
        out_specs=[
            pl.BlockSpec((ROWS, HEAD_DIM), lambda i, h: (i, h)),
            pl.BlockSpec((1, n_seg, 1, HEAD_DIM, HEAD_DIM), lambda i, h: (0, i, h, 0, 0)),
            pl.BlockSpec((n_seg, 1, 1, HEAD_DIM), lambda i, h: (i, h, 0, 0)),
            pl.BlockSpec((1, ROWS, 1), lambda i, h: (h, i, 0)),
        ],
        out_shape=[
            jax.ShapeDtypeStruct((m, D_MLSTM), BF16),
            jax.ShapeDtypeStruct(c0.shape, F32),
            jax.ShapeDtypeStruct(n0.shape, F32),
            jax.ShapeDtypeStruct(m_rows.shape, F32),
        ],
        compiler_params=_params(("parallel", "parallel")),
        name="mlstm_sample",
    )(proj, proj, proj, proj, gates, bias, m_rows, c0, n0)


def _ln_swish(y, g, b):
    mu = jnp.mean(y, axis=-1, keepdims=True)
    yc = y - mu
    var = jnp.mean(yc * yc, axis=-1, keepdims=True)
    z = yc * lax.rsqrt(var + EPS) * g + b
    return z * jax.nn.sigmoid(z)


def _conv_prompt_body(a_ref, b_ref, w_ref, bdw_ref, gln_ref, bln_ref, hc_ref, st_ref,
                      u_ref, us_ref, wb_ref, y_ref, *, tt):
    t_id = pl.program_id(1)
    n_shift = tt + CONV_PAD - 8

    @pl.when(t_id == 0)
    def _():
        u_ref[0:CONV_PAD, :] = jnp.zeros((CONV_PAD, D_CONV), F32)
        for s in range(CONV_WIDTH):
            wb_ref[s] = jnp.broadcast_to(w_ref[s:s + 1, :], (8, D_CONV))

    u_ref[CONV_PAD:CONV_PAD + tt, :] = a_ref[...] * jax.nn.sigmoid(b_ref[...])
    for r in range(1, 8):
        us_ref[r - 1] = u_ref[r:r + n_shift, :]

    def row_block(i, carry):
        base = pl.multiple_of(i * CONV_ROWS, CONV_ROWS)
        n_slab = CONV_ROWS // 8
        acc = [jnp.broadcast_to(bdw_ref[...], (8, D_CONV))] * n_slab
        for s in range(CONV_WIDTH):
            k8, r = divmod(CONV_PAD - HALO + s, 8)
            w = wb_ref[s]
            for j in range(n_slab):
                rows = pl.ds(pl.multiple_of(base + 8 * (k8 + j), 8), 8)
                win = u_ref[rows, :] if r == 0 else us_ref[r - 1, rows, :]
                acc[j] = acc[j] + w * win
        for j in range(n_slab):
            y_ref[pl.ds(pl.multiple_of(base + 8 * j, 8), 8), :] = acc[j]
        return carry

    lax.fori_loop(0, tt // CONV_ROWS, row_block, 0)
    hc_ref[...] = _ln_swish(y_ref[...], gln_ref[...], bln_ref[...]).astype(BF16)

    @pl.when(t_id == pl.num_programs(1) - 1)
    def _():
        st_ref[0, 0] = u_ref[CONV_PAD + tt - HALO:CONV_PAD + tt, :]

    u_ref[0:CONV_PAD, :] = u_ref[tt:tt + CONV_PAD, :]


def _conv_prompt(proj, w_dw, b_dw, g_ln, b_ln, batch, seq, *, tt=256):
    nt = seq // tt
    const = lambda b, t: (0, 0)
    return pl.pallas_call(
        functools.partial(_conv_prompt_body, tt=tt),
        grid=(batch, nt),
        in_specs=[
            pl.BlockSpec((tt, D_CONV), lambda b, t: (b * nt + t, 4)),
            pl.BlockSpec((tt, D_CONV), lambda b, t: (b * nt + t, 5)),
            pl.BlockSpec((CONV_WIDTH, D_CONV), const),
            pl.BlockSpec((1, D_CONV), const),
            pl.BlockSpec((1, D_CONV), const),
            pl.BlockSpec((1, D_CONV), const),
        ],
        out_specs=[
            pl.BlockSpec((tt, D_CONV), lambda b, t: (b * nt + t, 0)),
            pl.BlockSpec((1, 1, HALO, D_CONV), lambda b, t: (0, b, 0, 0)),
        ],
        out_shape=[
            jax.ShapeDtypeStruct((batch * seq, D_CONV), BF16),
            jax.ShapeDtypeStruct((1, batch, HALO, D_CONV), F32),
        ],
        scratch_shapes=[pltpu.VMEM((tt + CONV_PAD, D_CONV), F32),
                        pltpu.VMEM((7, tt + CONV_PAD - 8, D_CONV), F32),
                        pltpu.VMEM((CONV_WIDTH, 8, D_CONV), F32),
                        pltpu.VMEM((tt, D_CONV), F32)],
        compiler_params=_params(("parallel", "arbitrary")),
        name="conv_prompt",
    )(proj, proj, w_dw, b_dw, g_ln, b_ln)


def _conv_sample_body(a_ref, b_ref, st_ref, w_ref, bdw_ref, gln_ref, bln_ref, hc_ref, st_out_ref,
                      wb_ref, *, seq):
    @pl.when(pl.program_id(0) == 0)
    def _():
        for s in range(CONV_WIDTH):
            wb_ref[s] = jnp.broadcast_to(w_ref[s:s + 1, :], (8, D_CONV))

    acc = [jnp.broadcast_to(bdw_ref[...], (8, D_CONV))] * seq
    for j in range(HALO + seq):
        slab = st_ref[0, j] if j < HALO else a_ref[j - HALO] * jax.nn.sigmoid(b_ref[j - HALO])
        for t in range(seq):
            if 0 <= j - t < CONV_WIDTH:
                acc[t] = acc[t] + wb_ref[j - t] * slab
        if j >= seq:
            st_out_ref[0, j - seq] = slab
    for t in range(seq):
        hc_ref[t] = _ln_swish(acc[t], gln_ref[...], bln_ref[...])


def _conv_sample(a_t, b_t, state_t, w_dw, b_dw, g_ln, b_ln):
    seq, batch, _ = a_t.shape
    bb = 8
    const = lambda i: (0, 0)
    tok = pl.BlockSpec((seq, bb, D_CONV), lambda i: (0, i, 0))
    hist = pl.BlockSpec((1, HALO, bb, D_CONV), lambda i: (0, 0, i, 0))
    return pl.pallas_call(
        functools.partial(_conv_sample_body, seq=seq),
        grid=(batch // bb,),
        in_specs=[tok, tok, hist,
                  pl.BlockSpec((CONV_WIDTH, D_CONV), const),
                  pl.BlockSpec((1, D_CONV), const),
                  pl.BlockSpec((1, D_CONV), const),
                  pl.BlockSpec((1, D_CONV), const)],
        out_specs=[tok, hist],
        out_shape=[
            jax.ShapeDtypeStruct(a_t.shape, F32),
            jax.ShapeDtypeStruct(state_t.shape, F32),
        ],
        scratch_shapes=[pltpu.VMEM((CONV_WIDTH, 8, D_CONV), F32)],
        compiler_params=_params(("arbitrary",)),
        name="conv_sample",
    )(a_t, b_t, state_t, w_dw, b_dw, g_ln, b_ln)


def _proj_out_body(hm_ref, hc_ref, x_ref, wa_ref, wb_ref, g_ref, o_ref):
    mix = (jnp.dot(hm_ref[...], wa_ref[...], preferred_element_type=F32)
           + jnp.dot(hc_ref[...].astype(BF16), wb_ref[...], preferred_element_type=F32))
    o_ref[...] = x_ref[...] + _rms(mix, g_ref[...])


def _proj_out(hm, hc, x, w_out, g, *, tm=512):
    m = x.shape[0]
    row = lambda i: (i, 0)
    return pl.pallas_call(
        _proj_out_body,
        grid=(m // tm,),
        in_specs=[
            pl.BlockSpec((tm, D_MLSTM), row),
            pl.BlockSpec((tm, D_CONV), row),
            pl.BlockSpec((tm, D_MODEL), row),
            pl.BlockSpec((D_MLSTM, D_MODEL), lambda i: (0, 0)),
            pl.BlockSpec((D_CONV, D_MODEL), lambda i: (1, 0)),
            pl.BlockSpec((1, D_MODEL), lambda i: (0, 0)),
        ],
        out_specs=pl.BlockSpec((tm, D_MODEL), row),
        out_shape=jax.ShapeDtypeStruct((m, D_MODEL), F32),
        compiler_params=_params(("parallel",)),
        name="proj_out",
    )(hm, hc, x, w_out, w_out, g)


def kernel(x_prompt, x_sample, state_mlstm_C, state_mlstm_n, state_mlstm_m, state_conv, g_ffn1_pre, w_ffn1_gate, w_ffn1_up, w_ffn1_down, g_ffn1_post, g_mix_pre, w_in, b_igate, b_fgate, w_dw, b_dw, g_conv_ln, b_conv_ln, w_out, g_mix_post, g_ffn2_pre, w_ffn2_gate, w_ffn2_up, w_ffn2_down, g_ffn2_post):
    depth = state_mlstm_C.shape[0]
    assert depth == 1, "kernel handles a single layer"
    bp, tp, _ = x_prompt.shape
    bs, ts, _ = x_sample.shape
    l = 0

    w_in_t = jnp.swapaxes(w_in, 1, 2)
    w_gate_t = jnp.pad(w_in_t[l, D_MAIN:], ((0, LANES - 2 * N_HEADS), (0, 0))).astype(BF16)
    w_o = w_out[l].astype(BF16)
    bias = jnp.pad(jnp.concatenate([b_igate[l], b_fgate[l]]), (0, LANES - 2 * N_HEADS))[None, :]

    xs, *ffn1 = _ffn(x_sample.reshape(bs * ts, D_MODEL), g_ffn1_pre,
                     w_ffn1_gate[l], w_ffn1_up[l], w_ffn1_down[l], g_ffn1_post)
    proj_s, gates_s, w_main_t = _proj_in(xs, g_mix_pre, w_in_t, w_gate_t)
    m_rows = jnp.repeat(state_mlstm_m[l].T, ts, axis=1)[:, :, None]
    hm_s, c_s, n_s, m_s = _mlstm_sample(
        proj_s, gates_s, bias, m_rows, state_mlstm_C, state_mlstm_n[l][:, :, None, :], ts)
    glu_t = jnp.swapaxes(proj_s[:, 4 * D_MLSTM:].reshape(bs, ts, 2, D_CONV), 0, 1)
    hc_t, conv_t = _conv_sample(glu_t[:, :, 0], glu_t[:, :, 1], jnp.swapaxes(state_conv, 1, 2),
                                w_dw[l], b_dw, g_conv_ln, b_conv_ln)
    hc_s = jnp.swapaxes(hc_t, 0, 1).reshape(bs * ts, D_CONV)
    xs = _proj_out(hm_s, hc_s, xs, w_o, g_mix_post)
    ys, *ffn2 = _ffn(xs, g_ffn2_pre, w_ffn2_gate[l], w_ffn2_up[l], w_ffn2_down[l], g_ffn2_post)
    ys = ys.reshape(bs, ts, D_MODEL)

    xp = _ffn(x_prompt.reshape(bp * tp, D_MODEL), g_ffn1_pre, *ffn1, g_ffn1_post)[0]
    proj_p, gates_p = _proj_in(xp, g_mix_pre, w_main_t, w_gate_t)
    hm_p, c_p, n_p, m_p = _mlstm_prompt(proj_p, gates_p, bias, bp, tp)
    hc_p, conv_p = _conv_prompt(proj_p, w_dw[l], b_dw, g_conv_ln, b_conv_ln, bp, tp)
    xp = _proj_out(hm_p, hc_p, xp, w_o, g_mix_post)
    yp = _ffn(xp, g_ffn2_pre, *ffn2, g_ffn2_post)[0].reshape(bp, tp, D_MODEL)

    return (yp, ys,
            c_p[None], n_p[:, :, 0, :][None], m_p[:, :, 0, 0][None], conv_p,
            c_s, n_s[:, :, 0, :][None], m_s[:, ::ts, 0].T[None], jnp.swapaxes(conv_t, 1, 2))
```

```python
import functools

import jax
import jax.numpy as jnp
from jax import lax
from jax.experimental import pallas as pl
from jax.experimental.pallas import tpu as pltpu

D_MODEL = 2048
N_HEADS = 4
HEAD_DIM = 256
D_MLSTM = N_HEADS * HEAD_DIM
D_CONV = D_MODEL - D_MLSTM
CONV_WIDTH = 31
HALO = CONV_WIDTH - 1
CONV_PAD = 32
CONV_ROWS = 32
D_FF = 5632
FFN_TF = 512
D_MAIN = 4 * D_MLSTM + 2 * D_CONV
EPS = 1e-6
FFN_RES = 0.5
K_SCALE = HEAD_DIM ** -0.5

LANES = 128
ROWS = 128
VMEM_LIMIT = 56 * 1024 * 1024

F32 = jnp.float32
BF16 = jnp.bfloat16


def _params(sem):
    return pltpu.CompilerParams(dimension_semantics=sem, vmem_limit_bytes=VMEM_LIMIT)


def _rms(x, g):
    return x * lax.rsqrt(jnp.mean(x * x, axis=-1, keepdims=True) + EPS) * g


def _ffn_cast_body(x_ref, gpre_ref, wg_ref, wu_ref, wd_ref, gpost_ref, o_ref, wg_o, wu_o, wd_o,
                   h_ref, acc_ref):
    f = pl.program_id(1)

    @pl.when(f == 0)
    def _():
        h_ref[...] = _rms(x_ref[...], gpre_ref[...]).astype(BF16)
        acc_ref[...] = jnp.zeros_like(acc_ref)

    wg_o[0] = wg_ref[...].astype(BF16)
    wu_o[0] = wu_ref[...].astype(BF16)
    wd_o[...] = wd_ref[...].astype(BF16)
    h = h_ref[...]
    g = jnp.dot(h, wg_o[0], preferred_element_type=F32)
    u = jnp.dot(h, wu_o[0], preferred_element_type=F32)
    a = ((g * jax.nn.sigmoid(g)) * u).astype(BF16)
    acc_ref[...] += jnp.dot(a, wd_o[...], preferred_element_type=F32)

    @pl.when(f == pl.num_programs(1) - 1)
    def _():
        o_ref[...] = x_ref[...] + FFN_RES * _rms(acc_ref[...], gpost_ref[...])


def _ffn_skew_body(x_ref, gpre_ref, wg_ref, wu_ref, wd_ref, gpost_ref, o_ref,
                   h_ref, acc_ref, a0_ref, a1_ref, *, nf):
    f = pl.program_id(1)

    def gate_up(a_ref):
        h = h_ref[...]
        g = jnp.dot(h, wg_ref[0], preferred_element_type=F32)
        u = jnp.dot(h, wu_ref[0], preferred_element_type=F32)
        a_ref[...] = ((g * jax.nn.sigmoid(g)) * u).astype(BF16)

    def down(a_ref):
        acc_ref[...] += jnp.dot(a_ref[...], wd_ref[...], preferred_element_type=F32)

    @pl.when(f == 0)
    def _():
        h_ref[...] = _rms(x_ref[...], gpre_ref[...]).astype(BF16)
        acc_ref[...] = jnp.zeros_like(acc_ref)
        gate_up(a0_ref)

    @pl.when((f > 0) & (f < nf) & ((f & 1) == 1))
    def _():
        gate_up(a1_ref)
        down(a0_ref)

    @pl.when((f > 0) & (f < nf) & ((f & 1) == 0))
    def _():
        gate_up(a0_ref)
        down(a1_ref)

    @pl.when(f == nf)
    def _():
        down(a0_ref if (nf - 1) % 2 == 0 else a1_ref)
        o_ref[...] = x_ref[...] + FFN_RES * _rms(acc_ref[...], gpost_ref[...])


def _ffn(x, gpre, wg, wu, wd, gpost):
    m = x.shape[0]
    cast = wg.dtype == F32
    tm = min(m, 512)
    nf = D_FF // FFN_TF
    row = lambda i, f: (i, 0)
    const = lambda i, f: (0, 0)
    out_specs = [pl.BlockSpec((tm, D_MODEL), row)]
    out_shape = [jax.ShapeDtypeStruct((m, D_MODEL), F32)]
    if cast:
        assert m == tm, "the casting variant writes each weight tile once"
        tf = FFN_TF // 2
        body = _ffn_cast_body
        steps = D_FF // tf
        up_spec = pl.BlockSpec((D_MODEL, tf), lambda i, f: (0, f))
        down_spec = pl.BlockSpec((tf, D_MODEL), lambda i, f: (f, 0))
        up_out = pl.BlockSpec((1, D_MODEL, tf), lambda i, f: (f // 2, 0, f % 2))
        out_specs += [up_out, up_out, down_spec]
        chunked = jax.ShapeDtypeStruct((nf, D_MODEL, FFN_TF), BF16)
        out_shape += [chunked, chunked, jax.ShapeDtypeStruct(wd.shape, BF16)]
        scratch = []
    else:
        tf = FFN_TF
        body = functools.partial(_ffn_skew_body, nf=nf)
        steps = nf + 1
        up_spec = pl.BlockSpec((1, D_MODEL, tf), lambda i, f: (jnp.minimum(f, nf - 1), 0, 0))
        down_spec = pl.BlockSpec((tf, D_MODEL), lambda i, f: (jnp.maximum(f - 1, 0), 0))
        scratch = [pltpu.VMEM((tm, tf), BF16), pltpu.VMEM((tm, tf), BF16)]
    return pl.pallas_call(
        body,
        grid=(m // tm, steps),
        in_specs=[pl.BlockSpec((tm, D_MODEL), row), pl.BlockSpec((1, D_MODEL), const),
                  up_spec, up_spec, down_spec, pl.BlockSpec((1, D_MODEL), const)],
        out_specs=out_specs,
        out_shape=out_shape,
        scratch_shapes=[pltpu.VMEM((tm, D_MODEL), BF16), pltpu.VMEM((tm, D_MODEL), F32)] + scratch,
        compiler_params=_params(("parallel", "arbitrary")),
        name="ffn_cast" if cast else "ffn",
    )(x, gpre, wg, wu, wd, gpost)


def _proj_in_body(*refs, cast):
    if cast:
        x_ref, g_ref, w_ref, wgate_ref, proj_ref, gates_ref, w_o, h_ref = refs
    else:
        x_ref, g_ref, w_ref, wgate_ref, proj_ref, gates_ref, h_ref = refs
    nt = (((1,), (1,)), ((), ()))

    @pl.when(pl.program_id(1) == 0)
    def _():
        h = _rms(x_ref[...], g_ref[...]).astype(BF16)
        h_ref[...] = h
        gates_ref[...] = lax.dot_general(h, wgate_ref[...], nt, preferred_element_type=F32)

    if cast:
        w_o[...] = w_ref[0].astype(BF16)
        w_ref = w_o
    proj_ref[...] = lax.dot_general(h_ref[...], w_ref[...], nt, preferred_element_type=F32)


def _proj_in(x, g, w_main_t, w_gate_t):
    m = x.shape[0]
    cast = w_main_t.dtype == F32
    tm = min(m, 1024)
    tn = 1024 if cast else 1536
    if cast:
        assert m == tm, "the casting variant writes each weight tile once"
        w_spec = pl.BlockSpec((1, tn, D_MODEL), lambda i, n: (0, n, 0))
    else:
        w_spec = pl.BlockSpec((tn, D_MODEL), lambda i, n: (n, 0))
    out_specs = [pl.BlockSpec((tm, tn), lambda i, n: (i, n)), pl.BlockSpec((tm, LANES), lambda i, n: (i, 0))]
    out_shape = [jax.ShapeDtypeStruct((m, D_MAIN), F32), jax.ShapeDtypeStruct((m, LANES), F32)]
    if cast:
        out_specs.append(pl.BlockSpec((tn, D_MODEL), lambda i, n: (n, 0)))
        out_shape.append(jax.ShapeDtypeStruct((D_MAIN, D_MODEL), BF16))
    return pl.pallas_call(
        functools.partial(_proj_in_body, cast=cast),
        grid=(m // tm, D_MAIN // tn),
        in_specs=[
            pl.BlockSpec((tm, D_MODEL), lambda i, n: (i, 0)),
            pl.BlockSpec((1, D_MODEL), lambda i, n: (0, 0)),
            w_spec,
            pl.BlockSpec((LANES, D_MODEL), lambda i, n: (0, 0)),
        ],
        out_specs=out_specs,
        out_shape=out_shape,
        scratch_shapes=[pltpu.VMEM((tm, D_MODEL), BF16)],
        compiler_params=_params(("parallel", "arbitrary")),
        name="proj_in_cast" if cast else "proj_in",
    )(x, g, w_main_t, w_gate_t)


def _log_sigmoid(x):
    return jnp.minimum(x, 0.0) - jnp.log1p(jnp.exp(-jnp.abs(x)))


def _seg_cumsum(x, seg_len):
    pos = lax.broadcasted_iota(jnp.int32, x.shape, 0) & (seg_len - 1)
    shift = 1
    while shift < seg_len:
        x = x + jnp.where(pos >= shift, pltpu.roll(x, shift, 0), 0.0)
        shift *= 2
    return x


def _gate_terms(pre, seg_len):
    bt = _seg_cumsum(_log_sigmoid(pre), seg_len)
    return pre, bt, pre.T, bt.T


def _tile_masks(seg_len):
    log2 = seg_len.bit_length() - 1
    t_idx = lax.broadcasted_iota(jnp.int32, (ROWS, ROWS), 0)
    s_idx = lax.broadcasted_iota(jnp.int32, (ROWS, ROWS), 1)
    if seg_len == ROWS:
        return s_idx <= t_idx, None, None
    same = (t_idx >> log2) == (s_idx >> log2)
    last = s_idx == ((t_idx >> log2) << log2) + (seg_len - 1)
    return same & (s_idx <= t_idx), same, last


def _pick(x, j, axis):
    if isinstance(j, int):
        return x[:, j:j + 1] if axis == 1 else x[j:j + 1, :]
    idx = lax.broadcasted_iota(jnp.int32, x.shape, axis)
    return jnp.sum(jnp.where(idx == j, x, 0.0), axis=axis, keepdims=True)


def _mlstm_tile(q, k, v, gates, masks, hd, seg_len, m_prev):
    pre, bt_all, pre_t, bt_t = gates
    valid, same, last = masks
    ig_col = _pick(pre, hd, 1)
    bt_col = _pick(bt_all, hd + N_HEADS, 1)
    key_w = _pick(pre_t, hd, 0) - _pick(bt_t, hd + N_HEADS, 0)

    d = jnp.where(valid, bt_col + key_w, -jnp.inf)
    inter = bt_col + m_prev
    m_t = jnp.maximum(inter, jnp.max(d, axis=1, keepdims=True))
    w_intra = jnp.exp(d - m_t)
    w_inter = jnp.exp(inter - m_t)

    if seg_len == ROWS:
        bt_last = bt_col[ROWS - 1:ROWS, :]
        m_end = m_t[ROWS - 1:ROWS, :]
    else:
        bt_row = _pick(bt_t, hd + N_HEADS, 0)
        bt_last = jnp.sum(jnp.where(last, bt_row, 0.0), axis=1, keepdims=True)
        e = jnp.where(same, bt_last + key_w, -jnp.inf)
        m_end = jnp.maximum(bt_last + m_prev, jnp.max(e, axis=1, keepdims=True))
    g_keys = jnp.exp(bt_last - bt_col + ig_col - m_end)
    g_state = jnp.exp(bt_last + m_prev - m_end)

    ks = k * K_SCALE
    q_bf = q.astype(BF16)
    v_bf = v.astype(BF16)
    s = lax.dot_general(q_bf, ks.astype(BF16), (((1,), (1,)), ((), ())),
                        preferred_element_type=F32) * w_intra
    num = jnp.dot(s.astype(BF16), v_bf, preferred_element_type=F32)
    den = jnp.sum(s, axis=1, keepdims=True)
    kg = ks * g_keys
    return dict(q_bf=q_bf, v_bf=v_bf, num=num, den=den, kg=kg, m_t=m_t, w_inter=w_inter,
                g_state=g_state, m_end=m_end)


def _mlstm_out(t, q, o, q_c, q_n):
    num = t["num"] + q_c * t["w_inter"]
    den = t["den"] + q_n * t["w_inter"]
    h = num / jnp.maximum(jnp.abs(den), jnp.exp(-t["m_t"]))
    return (jax.nn.sigmoid(o) * h).astype(BF16)


def _mix_prompt_body(q_ref, k_ref, v_ref, o_ref, gates_ref, bias_ref, hc_ref, x_ref, wa_ref, wb_ref, g_ref,
                     y_ref, c_out_ref, n_out_ref, m_out_ref, c_ref, n_ref, m_ref, hm_ref, *, n_chunks, nt):
    t = pl.program_id(1)

    def project():
        mix = (jnp.dot(hm_ref[(t - 1) & 1], wa_ref[...], preferred_element_type=F32)
               + jnp.dot(hc_ref[...], wb_ref[...], preferred_element_type=F32))
        y_ref[...] = x_ref[...] + _rms(mix, g_ref[...])

    def recur():
        slot = t & 1
        masks = _tile_masks(ROWS)
        for c in range(n_chunks):
            rows = pl.ds(c * ROWS, ROWS)
            gates = _gate_terms(gates_ref[rows, :] + bias_ref[...], ROWS)
            for hd in range(N_HEADS):
                cols = pl.ds(hd * HEAD_DIM, HEAD_DIM)
                q = q_ref[rows, cols]
                tl = _mlstm_tile(q, k_ref[rows, cols], v_ref[rows, cols], gates, masks, hd, ROWS,
                                 m_ref[hd][:, 0:1])
                c_old = c_ref[hd]
                n_old = n_ref[hd]
                q_c = jnp.dot(tl["q_bf"], c_old.astype(BF16), preferred_element_type=F32)
                q_n = jnp.sum(q * n_old, axis=1, keepdims=True)
                hm_ref[slot, rows, cols] = _mlstm_out(tl, q, o_ref[rows, cols], q_c, q_n)
                g = tl["g_state"]
                c_ref[hd] = g * c_old + lax.dot_general(
                    tl["kg"].astype(BF16), tl["v_bf"], (((0,), (0,)), ((), ())), preferred_element_type=F32)
                n_ref[hd] = g * n_old + jnp.sum(tl["kg"], axis=0, keepdims=True)
                m_ref[hd] = jnp.broadcast_to(tl["m_end"], (1, LANES))

    @pl.when(t == 0)
    def _():
        c_ref[...] = jnp.zeros_like(c_ref)
        n_ref[...] = jnp.zeros_like(n_ref)
        m_ref[...] = jnp.zeros_like(m_ref)
        recur()

    @pl.when((t > 0) & (t < nt))
    def _():
        project()
        recur()

    @pl.when(t == nt)
    def _():
        project()
        c_out_ref[0] = c_ref[...]
        n_out_ref[0] = n_ref[...]
        m_out_ref[0] = m_ref[...]


def _mix_prompt(proj, gates, bias, hc, x, w_out, g, batch, seq, *, tt=256):
    nt = seq // tt
    cur = lambda j: (lambda b, t: (b * nt + jnp.minimum(t, nt - 1), j))
    prev = lambda b, t: (b * nt + jnp.maximum(t - 1, 0), 0)
    state = lambda b, t: (b, 0, 0, 0)
    const = lambda b, t: (0, 0)
    return pl.pallas_call(
        functools.partial(_mix_prompt_body, n_chunks=tt // ROWS, nt=nt),
        grid=(batch, nt + 1),
        in_specs=[
            pl.BlockSpec((tt, D_MLSTM), cur(0)),
            pl.BlockSpec((tt, D_MLSTM), cur(1)),
            pl.BlockSpec((tt, D_MLSTM), cur(2)),
            pl.BlockSpec((tt, D_MLSTM), cur(3)),
            pl.BlockSpec((tt, LANES), cur(0)),
            pl.BlockSpec((1, LANES), const),
            pl.BlockSpec((tt, D_CONV), prev),
            pl.BlockSpec((tt, D_MODEL), prev),
            pl.BlockSpec((D_MLSTM, D_MODEL), lambda b, t: (0, 0)),
            pl.BlockSpec((D_CONV, D_MODEL), lambda b, t: (1, 0)),
            pl.BlockSpec((1, D_MODEL), const),
        ],
        out_specs=[
            pl.BlockSpec((tt, D_MODEL), prev),
            pl.BlockSpec((1, N_HEADS, HEAD_DIM, HEAD_DIM), state),
            pl.BlockSpec((1, N_HEADS, 1, HEAD_DIM), state),
            pl.BlockSpec((1, N_HEADS, 1, LANES), state),
        ],
        out_shape=[
            jax.ShapeDtypeStruct((batch * seq, D_MODEL), F32),
            jax.ShapeDtypeStruct((batch, N_HEADS, HEAD_DIM, HEAD_DIM), F32),
            jax.ShapeDtypeStruct((batch, N_HEADS, 1, HEAD_DIM), F32),
            jax.ShapeDtypeStruct((batch, N_HEADS, 1, LANES), F32),
        ],
        scratch_shapes=[pltpu.VMEM((N_HEADS, HEAD_DIM, HEAD_DIM), F32), pltpu.VMEM((N_HEADS, 1, HEAD_DIM), F32),
                        pltpu.VMEM((N_HEADS, 1, LANES), F32), pltpu.VMEM((2, tt, D_MLSTM), BF16)],
        compiler_params=_params(("parallel", "arbitrary")),
        name="mix_prompt",
    )(proj, proj, proj, proj, gates, bias, hc, x, w_out, w_out, g)


def _mlstm_sample_body(q_ref, k_ref, v_ref, o_ref, gates_ref, bias_ref, mrow_ref, c_ref, n_ref,
                       hm_ref, c_out_ref, n_out_ref, m_out_ref, *, seg_len):
    hd = pl.program_id(1)
    n_seg = ROWS // seg_len
    grp = 16 // seg_len
    log2 = seg_len.bit_length() - 1
    q = q_ref[...]
    gates = _gate_terms(gates_ref[...] + bias_ref[...], seg_len)
    t = _mlstm_tile(q, k_ref[...], v_ref[...], gates, _tile_masks(seg_len), hd, seg_len, mrow_ref[0])

    seg_of_row = lax.broadcasted_iota(jnp.int32, (16, 1), 0) >> log2
    qc_parts, n_parts = [], []
    for j in range(ROWS // 16):
        qg = t["q_bf"][16 * j:16 * (j + 1)]
        qc, nr = None, None
        for i in range(grp):
            b = grp * j + i
            r = jnp.dot(qg, c_ref[0, b, 0].astype(BF16), preferred_element_type=F32)
            nb = jnp.broadcast_to(n_ref[b, 0], (16, HEAD_DIM))
            qc = r if i == 0 else jnp.where(seg_of_row == i, r, qc)
            nr = nb if i == 0 else jnp.where(seg_of_row == i, nb, nr)
        qc_parts.append(qc)
        n_parts.append(nr)
    q_c = jnp.concatenate(qc_parts, axis=0)
    q_n = jnp.sum(q * jnp.concatenate(n_parts, axis=0), axis=1, keepdims=True)
    hm_ref[...] = _mlstm_out(t, q, o_ref[...], q_c, q_n)

    kg = t["kg"]
    kg_t = kg.T
    seg_of_lane = lax.broadcasted_iota(jnp.int32, (1, ROWS), 1) >> log2
    seg_of_row8 = lax.broadcasted_iota(jnp.int32, (8, 1), 0) >> log2
    per8 = 8 // seg_len
    for b in range(n_seg):
        g = t["g_state"][seg_len * b:seg_len * b + 1, :]
        upd = jnp.dot(jnp.where(seg_of_lane == b, kg_t, 0.0).astype(BF16), t["v_bf"],
                      preferred_element_type=F32)
        c_out_ref[0, b, 0] = g * c_ref[0, b, 0] + upd
        kg8 = kg[8 * (b // per8):8 * (b // per8) + 8]
        n_out_ref[b, 0] = g * n_ref[b, 0] + jnp.sum(
            jnp.where(seg_of_row8 == (b % per8), kg8, 0.0), axis=0, keepdims=True)
    m_out_ref[0] = t["m_end"]


def _mlstm_sample(proj, gates, bias, m_rows, c0, n0, seg_len):
    m = proj.shape[0]
    n_seg = ROWS // seg_len
    col = lambda j: (lambda i, h: (i, j * N_HEADS + h))
    return pl.pallas_call(
        functools.partial(_mlstm_sample_body, seg_len=seg_len),
        grid=(m // ROWS, N_HEADS),
        in_specs=[
            pl.BlockSpec((ROWS, HEAD_DIM), col(0)),
            pl.BlockSpec((ROWS, HEAD_DIM), col(1)),
            pl.BlockSpec((ROWS, HEAD_DIM), col(2)),
            pl.BlockSpec((ROWS, HEAD_DIM), col(3)),
            pl.BlockSpec((ROWS, LANES), lambda i, h: (i, 0)),
            pl.BlockSpec((1, LANES), lambda i, h: (0, 0)),
            pl.BlockSpec((1, ROWS, 1), lambda i, h: (h, i, 0)),
            pl.BlockSpec((1, n_seg, 1, HEAD_DIM, HEAD_DIM), lambda i, h: (0, i, h, 0, 0)),
            pl.BlockSpec((n_seg, 1, 1, HEAD_DIM), lambda i, h: (i, h, 0, 0)),
        ],
        out_specs=[
            pl.BlockSpec((ROWS, HEAD_DIM), lambda i, h: (i, h)),
            pl.BlockSpec((1, n_seg, 1, HEAD_DIM, HEAD_DIM), lambda i, h: (0, i, h, 0, 0)),
            pl.BlockSpec((n_seg, 1, 1, HEAD_DIM), lambda i, h: (i, h, 0, 0)),
            pl.BlockSpec((1, ROWS, 1), lambda i, h: (h, i, 0)),
        ],
        out_shape=[
            jax.ShapeDtypeStruct((m, D_MLSTM), BF16),
            jax.ShapeDtypeStruct(c0.shape, F32),
            jax.ShapeDtypeStruct(n0.shape, F32),
            jax.ShapeDtypeStruct(m_rows.shape, F32),
        ],
        compiler_params=_params(("parallel", "parallel")),
        name="mlstm_sample",
    )(proj, proj, proj, proj, gates, bias, m_rows, c0, n0)


def _ln_swish(y, g, b):
    mu = jnp.mean(y, axis=-1, keepdims=True)
    yc = y - mu
    var = jnp.mean(yc * yc, axis=-1, keepdims=True)
    z = yc * lax.rsqrt(var + EPS) * g + b
    return z * jax.nn.sigmoid(z)


def _conv_prompt_body(a_ref, b_ref, w_ref, bdw_ref, gln_ref, bln_ref, hc_ref, st_ref,
                      u_ref, us_ref, wb_ref, y_ref, *, tt):
    t_id = pl.program_id(1)
    n_shift = tt + CONV_PAD - 8

    @pl.when(t_id == 0)
    def _():
        u_ref[0:CONV_PAD, :] = jnp.zeros((CONV_PAD, D_CONV), F32)
        for s in range(CONV_WIDTH):
            wb_ref[s] = jnp.broadcast_to(w_ref[s:s + 1, :], (8, D_CONV))

    u_ref[CONV_PAD:CONV_PAD + tt, :] = a_ref[...] * jax.nn.sigmoid(b_ref[...])
    for r in range(1, 8):
        us_ref[r - 1] = u_ref[r:r + n_shift, :]

    def row_block(i, carry):
        base = pl.multiple_of(i * CONV_ROWS, CONV_ROWS)
        n_slab = CONV_ROWS // 8
        acc = [jnp.broadcast_to(bdw_ref[...], (8, D_CONV))] * n_slab
        for s in range(CONV_WIDTH):
            k8, r = divmod(CONV_PAD - HALO + s, 8)
            w = wb_ref[s]
            for j in range(n_slab):
                rows = pl.ds(pl.multiple_of(base + 8 * (k8 + j), 8), 8)
                win = u_ref[rows, :] if r == 0 else us_ref[r - 1, rows, :]
                acc[j] = acc[j] + w * win
        for j in range(n_slab):
            y_ref[pl.ds(pl.multiple_of(base + 8 * j, 8), 8), :] = acc[j]
        return carry

    lax.fori_loop(0, tt // CONV_ROWS, row_block, 0)
    hc_ref[...] = _ln_swish(y_ref[...], gln_ref[...], bln_ref[...]).astype(BF16)

    @pl.when(t_id == pl.num_programs(1) - 1)
    def _():
        st_ref[0, 0] = u_ref[CONV_PAD + tt - HALO:CONV_PAD + tt, :]

    u_ref[0:CONV_PAD, :] = u_ref[tt:tt + CONV_PAD, :]


def _conv_prompt(proj, w_dw, b_dw, g_ln, b_ln, batch, seq, *, tt=256):
    nt = seq // tt
    const = lambda b, t: (0, 0)
    return pl.pallas_call(
        functools.partial(_conv_prompt_body, tt=tt),
        grid=(batch, nt),
        in_specs=[
            pl.BlockSpec((tt, D_CONV), lambda b, t: (b * nt + t, 4)),
            pl.BlockSpec((tt, D_CONV), lambda b, t: (b * nt + t, 5)),
            pl.BlockSpec((CONV_WIDTH, D_CONV), const),
            pl.BlockSpec((1, D_CONV), const),
            pl.BlockSpec((1, D_CONV), const),
            pl.BlockSpec((1, D_CONV), const),
        ],
        out_specs=[
            pl.BlockSpec((tt, D_CONV), lambda b, t: (b * nt + t, 0)),
            pl.BlockSpec((1, 1, HALO, D_CONV), lambda b, t: (0, b, 0, 0)),
        ],
        out_shape=[
            jax.ShapeDtypeStruct((batch * seq, D_CONV), BF16),
            jax.ShapeDtypeStruct((1, batch, HALO, D_CONV), F32),
        ],
        scratch_shapes=[pltpu.VMEM((tt + CONV_PAD, D_CONV), F32),
                        pltpu.VMEM((7, tt + CONV_PAD - 8, D_CONV), F32),
                        pltpu.VMEM((CONV_WIDTH, 8, D_CONV), F32),
                        pltpu.VMEM((tt, D_CONV), F32)],
        compiler_params=_params(("parallel", "arbitrary")),
        name="conv_prompt",
    )(proj, proj, w_dw, b_dw, g_ln, b_ln)


def _conv_sample_body(a_ref, b_ref, st_ref, w_ref, bdw_ref, gln_ref, bln_ref, hc_ref, st_out_ref,
                      wb_ref, *, seq):
    @pl.when(pl.program_id(0) == 0)
    def _():
        for s in range(CONV_WIDTH):
            wb_ref[s] = jnp.broadcast_to(w_ref[s:s + 1, :], (8, D_CONV))

    acc = [jnp.broadcast_to(bdw_ref[...], (8, D_CONV))] * seq
    for j in range(HALO + seq):
        slab = st_ref[0, j] if j < HALO else a_ref[j - HALO] * jax.nn.sigmoid(b_ref[j - HALO])
        for t in range(seq):
            if 0 <= j - t < CONV_WIDTH:
                acc[t] = acc[t] + wb_ref[j - t] * slab
        if j >= seq:
            st_out_ref[0, j - seq] = slab
    for t in range(seq):
        hc_ref[t] = _ln_swish(acc[t], gln_ref[...], bln_ref[...])


def _conv_sample(a_t, b_t, state_t, w_dw, b_dw, g_ln, b_ln):
    seq, batch, _ = a_t.shape
    bb = 8
    const = lambda i: (0, 0)
    tok = pl.BlockSpec((seq, bb, D_CONV), lambda i: (0, i, 0))
    hist = pl.BlockSpec((1, HALO, bb, D_CONV), lambda i: (0, 0, i, 0))
    return pl.pallas_call(
        functools.partial(_conv_sample_body, seq=seq),
        grid=(batch // bb,),
        in_specs=[tok, tok, hist,
                  pl.BlockSpec((CONV_WIDTH, D_CONV), const),
                  pl.BlockSpec((1, D_CONV), const),
                  pl.BlockSpec((1, D_CONV), const),
                  pl.BlockSpec((1, D_CONV), const)],
        out_specs=[tok, hist],
        out_shape=[
            jax.ShapeDtypeStruct(a_t.shape, F32),
            jax.ShapeDtypeStruct(state_t.shape, F32),
        ],
        scratch_shapes=[pltpu.VMEM((CONV_WIDTH, 8, D_CONV), F32)],
        compiler_params=_params(("arbitrary",)),
        name="conv_sample",
    )(a_t, b_t, state_t, w_dw, b_dw, g_ln, b_ln)


def _proj_out_body(hm_ref, hc_ref, x_ref, wa_ref, wb_ref, g_ref, o_ref):
    mix = (jnp.dot(hm_ref[...], wa_ref[...], preferred_element_type=F32)
           + jnp.dot(hc_ref[...].astype(BF16), wb_ref[...], preferred_element_type=F32))
    o_ref[...] = x_ref[...] + _rms(mix, g_ref[...])


def _proj_out(hm, hc, x, w_out, g, *, tm=512):
    m = x.shape[0]
    row = lambda i: (i, 0)
    return pl.pallas_call(
        _proj_out_body,
        grid=(m // tm,),
        in_specs=[
            pl.BlockSpec((tm, D_MLSTM), row),
            pl.BlockSpec((tm, D_CONV), row),
            pl.BlockSpec((tm, D_MODEL), row),
            pl.BlockSpec((D_MLSTM, D_MODEL), lambda i: (0, 0)),
            pl.BlockSpec((D_CONV, D_MODEL), lambda i: (1, 0)),
            pl.BlockSpec((1, D_MODEL), lambda i: (0, 0)),
        ],
        out_specs=pl.BlockSpec((tm, D_MODEL), row),
        out_shape=jax.ShapeDtypeStruct((m, D_MODEL), F32),
        compiler_params=_params(("parallel",)),
        name="proj_out",
    )(hm, hc, x, w_out, w_out, g)


def kernel(x_prompt, x_sample, state_mlstm_C, state_mlstm_n, state_mlstm_m, state_conv, g_ffn1_pre, w_ffn1_gate, w_ffn1_up, w_ffn1_down, g_ffn1_post, g_mix_pre, w_in, b_igate, b_fgate, w_dw, b_dw, g_conv_ln, b_conv_ln, w_out, g_mix_post, g_ffn2_pre, w_ffn2_gate, w_ffn2_up, w_ffn2_down, g_ffn2_post):
    depth = state_mlstm_C.shape[0]
    assert depth == 1, "kernel handles a single layer"
    bp, tp, _ = x_prompt.shape
    bs, ts, _ = x_sample.shape
    l = 0

    w_in_t = jnp.swapaxes(w_in, 1, 2)
    w_gate_t = jnp.pad(w_in_t[l, D_MAIN:], ((0, LANES - 2 * N_HEADS), (0, 0))).astype(BF16)
    w_o = w_out[l].astype(BF16)
    bias = jnp.pad(jnp.concatenate([b_igate[l], b_fgate[l]]), (0, LANES - 2 * N_HEADS))[None, :]

    xs, *ffn1 = _ffn(x_sample.reshape(bs * ts, D_MODEL), g_ffn1_pre,
                     w_ffn1_gate[l], w_ffn1_up[l], w_ffn1_down[l], g_ffn1_post)
    proj_s, gates_s, w_main_t = _proj_in(xs, g_mix_pre, w_in_t, w_gate_t)
    m_rows = jnp.repeat(state_mlstm_m[l].T, ts, axis=1)[:, :, None]
    hm_s, c_s, n_s, m_s = _mlstm_sample(
        proj_s, gates_s, bias, m_rows, state_mlstm_C, state_mlstm_n[l][:, :, None, :], ts)
    glu_t = jnp.swapaxes(proj_s[:, 4 * D_MLSTM:].reshape(bs, ts, 2, D_CONV), 0, 1)
    hc_t, conv_t = _conv_sample(glu_t[:, :, 0], glu_t[:, :, 1], jnp.swapaxes(state_conv, 1, 2),
                                w_dw[l], b_dw, g_conv_ln, b_conv_ln)
    hc_s = jnp.swapaxes(hc_t, 0, 1).reshape(bs * ts, D_CONV)
    xs = _proj_out(hm_s, hc_s, xs, w_o, g_mix_post)
    ys, *ffn2 = _ffn(xs, g_ffn2_pre, w_ffn2_gate[l], w_ffn2_up[l], w_ffn2_down[l], g_ffn2_post)
    ys = ys.reshape(bs, ts, D_MODEL)

    xp = _ffn(x_prompt.reshape(bp * tp, D_MODEL), g_ffn1_pre, *ffn1, g_ffn1_post)[0]
    proj_p, gates_p = _proj_in(xp, g_mix_pre, w_main_t, w_gate_t)
    hc_p, conv_p = _conv_prompt(proj_p, w_dw[l], b_dw, g_conv_ln, b_conv_ln, bp, tp)
    xp, c_p, n_p, m_p = _mix_prompt(proj_p, gates_p, bias, hc_p, xp, w_o, g_mix_post, bp, tp)
    yp = _ffn(xp, g_ffn2_pre, *ffn2, g_ffn2_post)[0].reshape(bp, tp, D_MODEL)

    return (yp, ys,
            c_p[None], n_p[:, :, 0, :][None], m_p[:, :, 0, 0][None], conv_p,
            c_s, n_s[:, :, 0, :][None], m_s[:, ::ts, 0].T[None], jnp.swapaxes(conv_t, 1, 2))
```

```python
import functools

import jax
import jax.numpy as jnp
from jax import lax
from jax.experimental import pallas as pl
from jax.experimental.pallas import tpu as pltpu

D_MODEL = 2048
N_HEADS = 4
HEAD_DIM = 256
D_MLSTM = N_HEADS * HEAD_DIM
D_CONV = D_MODEL - D_MLSTM
CONV_WIDTH = 31
HALO = CONV_WIDTH - 1
CONV_PAD = 32
CONV_ROWS = 32
D_FF = 5632
FFN_TF = 512
D_MAIN = 4 * D_MLSTM + 2 * D_CONV
EPS = 1e-6
FFN_RES = 0.5
K_SCALE = HEAD_DIM ** -0.5

LANES = 128
ROWS = 128
VMEM_LIMIT = 56 * 1024 * 1024

F32 = jnp.float32
BF16 = jnp.bfloat16


def _params(sem):
    return pltpu.CompilerParams(dimension_semantics=sem, vmem_limit_bytes=VMEM_LIMIT)


def _rms(x, g):
    return x * lax.rsqrt(jnp.mean(x * x, axis=-1, keepdims=True) + EPS) * g


def _ffn_cast_body(x_ref, gpre_ref, wg_ref, wu_ref, wd_ref, gpost_ref, o_ref, wg_o, wu_o, wd_o,
                   h_ref, acc_ref):
    f = pl.program_id(1)

    @pl.when(f == 0)
    def _():
        h_ref[...] = _rms(x_ref[...], gpre_ref[...]).astype(BF16)
        acc_ref[...] = jnp.zeros_like(acc_ref)

    wg_o[0] = wg_ref[...].astype(BF16)
    wu_o[0] = wu_ref[...].astype(BF16)
    wd_o[...] = wd_ref[...].astype(BF16)
    h = h_ref[...]
    g = jnp.dot(h, wg_o[0], preferred_element_type=F32)
    u = jnp.dot(h, wu_o[0], preferred_element_type=F32)
    a = ((g * jax.nn.sigmoid(g)) * u).astype(BF16)
    acc_ref[...] += jnp.dot(a, wd_o[...], preferred_element_type=F32)

    @pl.when(f == pl.num_programs(1) - 1)
    def _():
        o_ref[...] = x_ref[...] + FFN_RES * _rms(acc_ref[...], gpost_ref[...])


def _ffn_skew_body(x_ref, gpre_ref, wg_ref, wu_ref, wd_ref, gpost_ref, o_ref,
                   h_ref, a0_ref, a1_ref, *, nf):
    f = pl.program_id(1)

    def gate_up(a_ref):
        h = h_ref[...]
        g = jnp.dot(h, wg_ref[0], preferred_element_type=F32)
        u = jnp.dot(h, wu_ref[0], preferred_element_type=F32)
        a_ref[...] = ((g * jax.nn.sigmoid(g)) * u).astype(BF16)

    def down(a_ref):
        o_ref[...] += jnp.dot(a_ref[...], wd_ref[...], preferred_element_type=F32)

    @pl.when(f == 0)
    def _():
        h_ref[...] = _rms(x_ref[...], gpre_ref[...]).astype(BF16)
        o_ref[...] = jnp.zeros_like(o_ref)
        gate_up(a0_ref)

    @pl.when((f > 0) & (f < nf) & ((f & 1) == 1))
    def _():
        gate_up(a1_ref)
        down(a0_ref)

    @pl.when((f > 0) & (f < nf) & ((f & 1) == 0))
    def _():
        gate_up(a0_ref)
        down(a1_ref)

    @pl.when(f == nf)
    def _():
        down(a0_ref if (nf - 1) % 2 == 0 else a1_ref)
        o_ref[...] = x_ref[...] + FFN_RES * _rms(o_ref[...], gpost_ref[...])


def _ffn(x, gpre, wg, wu, wd, gpost):
    m = x.shape[0]
    cast = wg.dtype == F32
    tm = min(m, 512 if cast else 1024)
    nf = D_FF // FFN_TF
    row = lambda i, f: (i, 0)
    const = lambda i, f: (0, 0)
    out_specs = [pl.BlockSpec((tm, D_MODEL), row)]
    out_shape = [jax.ShapeDtypeStruct((m, D_MODEL), F32)]
    if cast:
        assert m == tm, "the casting variant writes each weight tile once"
        tf = FFN_TF // 2
        body = _ffn_cast_body
        steps = D_FF // tf
        up_spec = pl.BlockSpec((D_MODEL, tf), lambda i, f: (0, f))
        down_spec = pl.BlockSpec((tf, D_MODEL), lambda i, f: (f, 0))
        up_out = pl.BlockSpec((1, D_MODEL, tf), lambda i, f: (f // 2, 0, f % 2))
        out_specs += [up_out, up_out, down_spec]
        chunked = jax.ShapeDtypeStruct((nf, D_MODEL, FFN_TF), BF16)
        out_shape += [chunked, chunked, jax.ShapeDtypeStruct(wd.shape, BF16)]
        scratch = [pltpu.VMEM((tm, D_MODEL), F32)]
    else:
        tf = FFN_TF
        body = functools.partial(_ffn_skew_body, nf=nf)
        steps = nf + 1
        up_spec = pl.BlockSpec((1, D_MODEL, tf), lambda i, f: (jnp.minimum(f, nf - 1), 0, 0))
        down_spec = pl.BlockSpec((tf, D_MODEL), lambda i, f: (jnp.maximum(f - 1, 0), 0))
        scratch = [pltpu.VMEM((tm, tf), BF16), pltpu.VMEM((tm, tf), BF16)]
    return pl.pallas_call(
        body,
        grid=(m // tm, steps),
        in_specs=[pl.BlockSpec((tm, D_MODEL), row), pl.BlockSpec((1, D_MODEL), const),
                  up_spec, up_spec, down_spec, pl.BlockSpec((1, D_MODEL), const)],
        out_specs=out_specs,
        out_shape=out_shape,
        scratch_shapes=[pltpu.VMEM((tm, D_MODEL), BF16)] + scratch,
        compiler_params=_params(("parallel", "arbitrary")),
        name="ffn_cast" if cast else "ffn",
    )(x, gpre, wg, wu, wd, gpost)


def _proj_in_body(*refs, cast):
    if cast:
        x_ref, g_ref, w_ref, wgate_ref, proj_ref, gates_ref, w_o, h_ref = refs
    else:
        x_ref, g_ref, w_ref, wgate_ref, proj_ref, gates_ref, h_ref = refs
    nt = (((1,), (1,)), ((), ()))

    @pl.when(pl.program_id(1) == 0)
    def _():
        h = _rms(x_ref[...], g_ref[...]).astype(BF16)
        h_ref[...] = h
        gates_ref[...] = lax.dot_general(h, wgate_ref[...], nt, preferred_element_type=F32)

    if cast:
        w_o[...] = w_ref[0].astype(BF16)
        w_ref = w_o
    proj_ref[...] = lax.dot_general(h_ref[...], w_ref[...], nt, preferred_element_type=F32)


def _proj_in(x, g, w_main_t, w_gate_t):
    m = x.shape[0]
    cast = w_main_t.dtype == F32
    tm = min(m, 1024)
    tn = 1024 if cast else 1536
    if cast:
        assert m == tm, "the casting variant writes each weight tile once"
        w_spec = pl.BlockSpec((1, tn, D_MODEL), lambda i, n: (0, n, 0))
    else:
        w_spec = pl.BlockSpec((tn, D_MODEL), lambda i, n: (n, 0))
    out_specs = [pl.BlockSpec((tm, tn), lambda i, n: (i, n)), pl.BlockSpec((tm, LANES), lambda i, n: (i, 0))]
    out_shape = [jax.ShapeDtypeStruct((m, D_MAIN), F32), jax.ShapeDtypeStruct((m, LANES), F32)]
    if cast:
        out_specs.append(pl.BlockSpec((tn, D_MODEL), lambda i, n: (n, 0)))
        out_shape.append(jax.ShapeDtypeStruct((D_MAIN, D_MODEL), BF16))
    return pl.pallas_call(
        functools.partial(_proj_in_body, cast=cast),
        grid=(m // tm, D_MAIN // tn),
        in_specs=[
            pl.BlockSpec((tm, D_MODEL), lambda i, n: (i, 0)),
            pl.BlockSpec((1, D_MODEL), lambda i, n: (0, 0)),
            w_spec,
            pl.BlockSpec((LANES, D_MODEL), lambda i, n: (0, 0)),
        ],
        out_specs=out_specs,
        out_shape=out_shape,
        scratch_shapes=[pltpu.VMEM((tm, D_MODEL), BF16)],
        compiler_params=_params(("parallel", "arbitrary")),
        name="proj_in_cast" if cast else "proj_in",
    )(x, g, w_main_t, w_gate_t)


def _log_sigmoid(x):
    return jnp.minimum(x, 0.0) - jnp.log1p(jnp.exp(-jnp.abs(x)))


def _seg_cumsum(x, seg_len):
    pos = lax.broadcasted_iota(jnp.int32, x.shape, 0) & (seg_len - 1)
    shift = 1
    while shift < seg_len:
        x = x + jnp.where(pos >= shift, pltpu.roll(x, shift, 0), 0.0)
        shift *= 2
    return x


def _gate_terms(pre, seg_len):
    bt = _seg_cumsum(_log_sigmoid(pre), seg_len)
    return pre, bt, pre.T, bt.T


def _tile_masks(seg_len):
    log2 = seg_len.bit_length() - 1
    t_idx = lax.broadcasted_iota(jnp.int32, (ROWS, ROWS), 0)
    s_idx = lax.broadcasted_iota(jnp.int32, (ROWS, ROWS), 1)
    if seg_len == ROWS:
        return s_idx <= t_idx, None, None
    same = (t_idx >> log2) == (s_idx >> log2)
    last = s_idx == ((t_idx >> log2) << log2) + (seg_len - 1)
    return same & (s_idx <= t_idx), same, last


def _pick(x, j, axis):
    if isinstance(j, int):
        return x[:, j:j + 1] if axis == 1 else x[j:j + 1, :]
    idx = lax.broadcasted_iota(jnp.int32, x.shape, axis)
    return jnp.sum(jnp.where(idx == j, x, 0.0), axis=axis, keepdims=True)


def _mlstm_tile(q, k, v, gates, masks, hd, seg_len, m_prev):
    pre, bt_all, pre_t, bt_t = gates
    valid, same, last = masks
    ig_col = _pick(pre, hd, 1)
    bt_col = _pick(bt_all, hd + N_HEADS, 1)
    key_w = _pick(pre_t, hd, 0) - _pick(bt_t, hd + N_HEADS, 0)

    d = jnp.where(valid, bt_col + key_w, -jnp.inf)
    inter = bt_col + m_prev
    m_t = jnp.maximum(inter, jnp.max(d, axis=1, keepdims=True))
    w_intra = jnp.exp(d - m_t)
    w_inter = jnp.exp(inter - m_t)

    if seg_len == ROWS:
        bt_last = bt_col[ROWS - 1:ROWS, :]
        m_end = m_t[ROWS - 1:ROWS, :]
    else:
        bt_row = _pick(bt_t, hd + N_HEADS, 0)
        bt_last = jnp.sum(jnp.where(last, bt_row, 0.0), axis=1, keepdims=True)
        e = jnp.where(same, bt_last + key_w, -jnp.inf)
        m_end = jnp.maximum(bt_last + m_prev, jnp.max(e, axis=1, keepdims=True))
    g_keys = jnp.exp(bt_last - bt_col + ig_col - m_end)
    g_state = jnp.exp(bt_last + m_prev - m_end)

    ks = k * K_SCALE
    q_bf = q.astype(BF16)
    v_bf = v.astype(BF16)
    s = lax.dot_general(q_bf, ks.astype(BF16), (((1,), (1,)), ((), ())),
                        preferred_element_type=F32) * w_intra
    num = jnp.dot(s.astype(BF16), v_bf, preferred_element_type=F32)
    den = jnp.sum(s, axis=1, keepdims=True)
    kg = ks * g_keys
    return dict(q_bf=q_bf, v_bf=v_bf, num=num, den=den, kg=kg, m_t=m_t, w_inter=w_inter,
                g_state=g_state, m_end=m_end)


def _mlstm_out(t, q, o, q_c, q_n):
    num = t["num"] + q_c * t["w_inter"]
    den = t["den"] + q_n * t["w_inter"]
    h = num / jnp.maximum(jnp.abs(den), jnp.exp(-t["m_t"]))
    return (jax.nn.sigmoid(o) * h).astype(BF16)


def _mix_prompt_body(q_ref, k_ref, v_ref, o_ref, gates_ref, bias_ref, hc_ref, x_ref, wa_ref, wb_ref, g_ref,
                     y_ref, c_out_ref, n_out_ref, m_out_ref, c_ref, n_ref, m_ref, hm_ref, *, n_chunks, nt):
    t = pl.program_id(1)

    def project():
        mix = (jnp.dot(hm_ref[(t - 1) & 1], wa_ref[...], preferred_element_type=F32)
               + jnp.dot(hc_ref[...], wb_ref[...], preferred_element_type=F32))
        y_ref[...] = x_ref[...] + _rms(mix, g_ref[...])

    def recur():
        slot = t & 1
        masks = _tile_masks(ROWS)
        for c in range(n_chunks):
            rows = pl.ds(c * ROWS, ROWS)
            gates = _gate_terms(gates_ref[rows, :] + bias_ref[...], ROWS)
            for hd in range(N_HEADS):
                cols = pl.ds(hd * HEAD_DIM, HEAD_DIM)
                q = q_ref[rows, cols]
                tl = _mlstm_tile(q, k_ref[rows, cols], v_ref[rows, cols], gates, masks, hd, ROWS,
                                 m_ref[hd][:, 0:1])
                c_old = c_ref[hd]
                n_old = n_ref[hd]
                q_c = jnp.dot(tl["q_bf"], c_old.astype(BF16), preferred_element_type=F32)
                q_n = jnp.sum(q * n_old, axis=1, keepdims=True)
                hm_ref[slot, rows, cols] = _mlstm_out(tl, q, o_ref[rows, cols], q_c, q_n)
                g = tl["g_state"]
                c_ref[hd] = g * c_old + lax.dot_general(
                    tl["kg"].astype(BF16), tl["v_bf"], (((0,), (0,)), ((), ())), preferred_element_type=F32)
                n_ref[hd] = g * n_old + jnp.sum(tl["kg"], axis=0, keepdims=True)
                m_ref[hd] = jnp.broadcast_to(tl["m_end"], (1, LANES))

    @pl.when(t == 0)
    def _():
        c_ref[...] = jnp.zeros_like(c_ref)
        n_ref[...] = jnp.zeros_like(n_ref)
        m_ref[...] = jnp.zeros_like(m_ref)
        recur()

    @pl.when((t > 0) & (t < nt))
    def _():
        project()
        recur()

    @pl.when(t == nt)
    def _():
        project()
        c_out_ref[0] = c_ref[...]
        n_out_ref[0] = n_ref[...]
        m_out_ref[0] = m_ref[...]


def _mix_prompt(proj, gates, bias, hc, x, w_out, g, batch, seq, *, tt=256):
    nt = seq // tt
    cur = lambda j: (lambda b, t: (b * nt + jnp.minimum(t, nt - 1), j))
    prev = lambda b, t: (b * nt + jnp.maximum(t - 1, 0), 0)
    state = lambda b, t: (b, 0, 0, 0)
    const = lambda b, t: (0, 0)
    return pl.pallas_call(
        functools.partial(_mix_prompt_body, n_chunks=tt // ROWS, nt=nt),
        grid=(batch, nt + 1),
        in_specs=[
            pl.BlockSpec((tt, D_MLSTM), cur(0)),
            pl.BlockSpec((tt, D_MLSTM), cur(1)),
            pl.BlockSpec((tt, D_MLSTM), cur(2)),
            pl.BlockSpec((tt, D_MLSTM), cur(3)),
            pl.BlockSpec((tt, LANES), cur(0)),
            pl.BlockSpec((1, LANES), const),
            pl.BlockSpec((tt, D_CONV), prev),
            pl.BlockSpec((tt, D_MODEL), prev),
            pl.BlockSpec((D_MLSTM, D_MODEL), lambda b, t: (0, 0)),
            pl.BlockSpec((D_CONV, D_MODEL), lambda b, t: (1, 0)),
            pl.BlockSpec((1, D_MODEL), const),
        ],
        out_specs=[
            pl.BlockSpec((tt, D_MODEL), prev),
            pl.BlockSpec((1, N_HEADS, HEAD_DIM, HEAD_DIM), state),
            pl.BlockSpec((1, N_HEADS, 1, HEAD_DIM), state),
            pl.BlockSpec((1, N_HEADS, 1, LANES), state),
        ],
        out_shape=[
            jax.ShapeDtypeStruct((batch * seq, D_MODEL), F32),
            jax.ShapeDtypeStruct((batch, N_HEADS, HEAD_DIM, HEAD_DIM), F32),
            jax.ShapeDtypeStruct((batch, N_HEADS, 1, HEAD_DIM), F32),
            jax.ShapeDtypeStruct((batch, N_HEADS, 1, LANES), F32),
        ],
        scratch_shapes=[pltpu.VMEM((N_HEADS, HEAD_DIM, HEAD_DIM), F32), pltpu.VMEM((N_HEADS, 1, HEAD_DIM), F32),
                        pltpu.VMEM((N_HEADS, 1, LANES), F32), pltpu.VMEM((2, tt, D_MLSTM), BF16)],
        compiler_params=_params(("parallel", "arbitrary")),
        name="mix_prompt",
    )(proj, proj, proj, proj, gates, bias, hc, x, w_out, w_out, g)


def _mlstm_sample_body(q_ref, k_ref, v_ref, o_ref, gates_ref, bias_ref, mrow_ref, c_ref, n_ref,
                       hm_ref, c_out_ref, n_out_ref, m_out_ref, *, seg_len):
    hd = pl.program_id(1)
    n_seg = ROWS // seg_len
    grp = 16 // seg_len
    log2 = seg_len.bit_length() - 1
    q = q_ref[...]
    gates = _gate_terms(gates_ref[...] + bias_ref[...], seg_len)
    t = _mlstm_tile(q, k_ref[...], v_ref[...], gates, _tile_masks(seg_len), hd, seg_len, mrow_ref[0])

    seg_of_row = lax.broadcasted_iota(jnp.int32, (16, 1), 0) >> log2
    qc_parts, n_parts = [], []
    for j in range(ROWS // 16):
        qg = t["q_bf"][16 * j:16 * (j + 1)]
        qc, nr = None, None
        for i in range(grp):
            b = grp * j + i
            r = jnp.dot(qg, c_ref[0, b, 0].astype(BF16), preferred_element_type=F32)
            nb = jnp.broadcast_to(n_ref[b, 0], (16, HEAD_DIM))
            qc = r if i == 0 else jnp.where(seg_of_row == i, r, qc)
            nr = nb if i == 0 else jnp.where(seg_of_row == i, nb, nr)
        qc_parts.append(qc)
        n_parts.append(nr)
    q_c = jnp.concatenate(qc_parts, axis=0)
    q_n = jnp.sum(q * jnp.concatenate(n_parts, axis=0), axis=1, keepdims=True)
    hm_ref[...] = _mlstm_out(t, q, o_ref[...], q_c, q_n)

    kg = t["kg"]
    kg_t = kg.T
    seg_of_lane = lax.broadcasted_iota(jnp.int32, (1, ROWS), 1) >> log2
    seg_of_row8 = lax.broadcasted_iota(jnp.int32, (8, 1), 0) >> log2
    per8 = 8 // seg_len
    for b in range(n_seg):
        g = t["g_state"][seg_len * b:seg_len * b + 1, :]
        upd = jnp.dot(jnp.where(seg_of_lane == b, kg_t, 0.0).astype(BF16), t["v_bf"],
                      preferred_element_type=F32)
        c_out_ref[0, b, 0] = g * c_ref[0, b, 0] + upd
        kg8 = kg[8 * (b // per8):8 * (b // per8) + 8]
        n_out_ref[b, 0] = g * n_ref[b, 0] + jnp.sum(
            jnp.where(seg_of_row8 == (b % per8), kg8, 0.0), axis=0, keepdims=True)
    m_out_ref[0] = t["m_end"]


def _mlstm_sample(proj, gates, bias, m_rows, c0, n0, seg_len):
    m = proj.shape[0]
    n_seg = ROWS // seg_len
    col = lambda j: (lambda i, h: (i, j * N_HEADS + h))
    return pl.pallas_call(
        functools.partial(_mlstm_sample_body, seg_len=seg_len),
        grid=(m // ROWS, N_HEADS),
        in_specs=[
            pl.BlockSpec((ROWS, HEAD_DIM), col(0)),
            pl.BlockSpec((ROWS, HEAD_DIM), col(1)),
            pl.BlockSpec((ROWS, HEAD_DIM), col(2)),
            pl.BlockSpec((ROWS, HEAD_DIM), col(3)),
            pl.BlockSpec((ROWS, LANES), lambda i, h: (i, 0)),
            pl.BlockSpec((1, LANES), lambda i, h: (0, 0)),
            pl.BlockSpec((1, ROWS, 1), lambda i, h: (h, i, 0)),
            pl.BlockSpec((1, n_seg, 1, HEAD_DIM, HEAD_DIM), lambda i, h: (0, i, h, 0, 0)),
            pl.BlockSpec((n_seg, 1, 1, HEAD_DIM), lambda i, h: (i, h, 0, 0)),
        ],
        out_specs=[
            pl.BlockSpec((ROWS, HEAD_DIM), lambda i, h: (i, h)),
            pl.BlockSpec((1, n_seg, 1, HEAD_DIM, HEAD_DIM), lambda i, h: (0, i, h, 0, 0)),
            pl.BlockSpec((n_seg, 1, 1, HEAD_DIM), lambda i, h: (i, h, 0, 0)),
            pl.BlockSpec((1, ROWS, 1), lambda i, h: (h, i, 0)),
        ],
        out_shape=[
            jax.ShapeDtypeStruct((m, D_MLSTM), BF16),
            jax.ShapeDtypeStruct(c0.shape, F32),
            jax.ShapeDtypeStruct(n0.shape, F32),
            jax.ShapeDtypeStruct(m_rows.shape, F32),
        ],
        compiler_params=_params(("parallel", "parallel")),
        name="mlstm_sample",
    )(proj, proj, proj, proj, gates, bias, m_rows, c0, n0)


def _ln_swish(y, g, b):
    mu = jnp.mean(y, axis=-1, keepdims=True)
    yc = y - mu
    var = jnp.mean(yc * yc, axis=-1, keepdims=True)
    z = yc * lax.rsqrt(var + EPS) * g + b
    return z * jax.nn.sigmoid(z)


def _conv_prompt_body(a_ref, b_ref, w_ref, bdw_ref, gln_ref, bln_ref, hc_ref, st_ref,
                      u_ref, us_ref, wb_ref, y_ref, *, tt):
    t_id = pl.program_id(1)
    n_shift = tt + CONV_PAD - 8

    @pl.when(t_id == 0)
    def _():
        u_ref[0:CONV_PAD, :] = jnp.zeros((CONV_PAD, D_CONV), F32)
        for s in range(CONV_WIDTH):
            wb_ref[s] = jnp.broadcast_to(w_ref[s:s + 1, :], (8, D_CONV))

    u_ref[CONV_PAD:CONV_PAD + tt, :] = a_ref[...] * jax.nn.sigmoid(b_ref[...])
    for r in range(1, 8):
        us_ref[r - 1] = u_ref[r:r + n_shift, :]

    def row_block(i, carry):
        base = pl.multiple_of(i * CONV_ROWS, CONV_ROWS)
        n_slab = CONV_ROWS // 8
        acc = [jnp.broadcast_to(bdw_ref[...], (8, D_CONV))] * n_slab
        for s in range(CONV_WIDTH):
            k8, r = divmod(CONV_PAD - HALO + s, 8)
            w = wb_ref[s]
            for j in range(n_slab):
                rows = pl.ds(pl.multiple_of(base + 8 * (k8 + j), 8), 8)
                win = u_ref[rows, :] if r == 0 else us_ref[r - 1, rows, :]
                acc[j] = acc[j] + w * win
        for j in range(n_slab):
            y_ref[pl.ds(pl.multiple_of(base + 8 * j, 8), 8), :] = acc[j]
        return carry

    lax.fori_loop(0, tt // CONV_ROWS, row_block, 0)
    hc_ref[...] = _ln_swish(y_ref[...], gln_ref[...], bln_ref[...]).astype(BF16)

    @pl.when(t_id == pl.num_programs(1) - 1)
    def _():
        st_ref[0, 0] = u_ref[CONV_PAD + tt - HALO:CONV_PAD + tt, :]

    u_ref[0:CONV_PAD, :] = u_ref[tt:tt + CONV_PAD, :]


def _conv_prompt(proj, w_dw, b_dw, g_ln, b_ln, batch, seq, *, tt=256):
    nt = seq // tt
    const = lambda b, t: (0, 0)
    return pl.pallas_call(
        functools.partial(_conv_prompt_body, tt=tt),
        grid=(batch, nt),
        in_specs=[
            pl.BlockSpec((tt, D_CONV), lambda b, t: (b * nt + t, 4)),
            pl.BlockSpec((tt, D_CONV), lambda b, t: (b * nt + t, 5)),
            pl.BlockSpec((CONV_WIDTH, D_CONV), const),
            pl.BlockSpec((1, D_CONV), const),
            pl.BlockSpec((1, D_CONV), const),
            pl.BlockSpec((1, D_CONV), const),
        ],
        out_specs=[
            pl.BlockSpec((tt, D_CONV), lambda b, t: (b * nt + t, 0)),
            pl.BlockSpec((1, 1, HALO, D_CONV), lambda b, t: (0, b, 0, 0)),
        ],
        out_shape=[
            jax.ShapeDtypeStruct((batch * seq, D_CONV), BF16),
            jax.ShapeDtypeStruct((1, batch, HALO, D_CONV), F32),
        ],
        scratch_shapes=[pltpu.VMEM((tt + CONV_PAD, D_CONV), F32),
                        pltpu.VMEM((7, tt + CONV_PAD - 8, D_CONV), F32),
                        pltpu.VMEM((CONV_WIDTH, 8, D_CONV), F32),
                        pltpu.VMEM((tt, D_CONV), F32)],
        compiler_params=_params(("parallel", "arbitrary")),
        name="conv_prompt",
    )(proj, proj, w_dw, b_dw, g_ln, b_ln)


def _conv_sample_body(a_ref, b_ref, st_ref, w_ref, bdw_ref, gln_ref, bln_ref, hc_ref, st_out_ref,
                      wb_ref, *, seq):
    @pl.when(pl.program_id(0) == 0)
    def _():
        for s in range(CONV_WIDTH):
            wb_ref[s] = jnp.broadcast_to(w_ref[s:s + 1, :], (8, D_CONV))

    acc = [jnp.broadcast_to(bdw_ref[...], (8, D_CONV))] * seq
    for j in range(HALO + seq):
        slab = st_ref[0, j] if j < HALO else a_ref[j - HALO] * jax.nn.sigmoid(b_ref[j - HALO])
        for t in range(seq):
            if 0 <= j - t < CONV_WIDTH:
                acc[t] = acc[t] + wb_ref[j - t] * slab
        if j >= seq:
            st_out_ref[0, j - seq] = slab
    for t in range(seq):
        hc_ref[t] = _ln_swish(acc[t], gln_ref[...], bln_ref[...])


def _conv_sample(a_t, b_t, state_t, w_dw, b_dw, g_ln, b_ln):
    seq, batch, _ = a_t.shape
    bb = 8
    const = lambda i: (0, 0)
    tok = pl.BlockSpec((seq, bb, D_CONV), lambda i: (0, i, 0))
    hist = pl.BlockSpec((1, HALO, bb, D_CONV), lambda i: (0, 0, i, 0))
    return pl.pallas_call(
        functools.partial(_conv_sample_body, seq=seq),
        grid=(batch // bb,),
        in_specs=[tok, tok, hist,
                  pl.BlockSpec((CONV_WIDTH, D_CONV), const),
                  pl.BlockSpec((1, D_CONV), const),
                  pl.BlockSpec((1, D_CONV), const),
                  pl.BlockSpec((1, D_CONV), const)],
        out_specs=[tok, hist],
        out_shape=[
            jax.ShapeDtypeStruct(a_t.shape, F32),
            jax.ShapeDtypeStruct(state_t.shape, F32),
        ],
        scratch_shapes=[pltpu.VMEM((CONV_WIDTH, 8, D_CONV), F32)],
        compiler_params=_params(("arbitrary",)),
        name="conv_sample",
    )(a_t, b_t, state_t, w_dw, b_dw, g_ln, b_ln)


def _proj_out_body(hm_ref, hc_ref, x_ref, wa_ref, wb_ref, g_ref, o_ref):
    mix = (jnp.dot(hm_ref[...], wa_ref[...], preferred_element_type=F32)
           + jnp.dot(hc_ref[...].astype(BF16), wb_ref[...], preferred_element_type=F32))
    o_ref[...] = x_ref[...] + _rms(mix, g_ref[...])


def _proj_out(hm, hc, x, w_out, g, *, tm=512):
    m = x.shape[0]
    row = lambda i: (i, 0)
    return pl.pallas_call(
        _proj_out_body,
        grid=(m // tm,),
        in_specs=[
            pl.BlockSpec((tm, D_MLSTM), row),
            pl.BlockSpec((tm, D_CONV), row),
            pl.BlockSpec((tm, D_MODEL), row),
            pl.BlockSpec((D_MLSTM, D_MODEL), lambda i: (0, 0)),
            pl.BlockSpec((D_CONV, D_MODEL), lambda i: (1, 0)),
            pl.BlockSpec((1, D_MODEL), lambda i: (0, 0)),
        ],
        out_specs=pl.BlockSpec((tm, D_MODEL), row),
        out_shape=jax.ShapeDtypeStruct((m, D_MODEL), F32),
        compiler_params=_params(("parallel",)),
        name="proj_out",
    )(hm, hc, x, w_out, w_out, g)


def kernel(x_prompt, x_sample, state_mlstm_C, state_mlstm_n, state_mlstm_m, state_conv, g_ffn1_pre, w_ffn1_gate, w_ffn1_up, w_ffn1_down, g_ffn1_post, g_mix_pre, w_in, b_igate, b_fgate, w_dw, b_dw, g_conv_ln, b_conv_ln, w_out, g_mix_post, g_ffn2_pre, w_ffn2_gate, w_ffn2_up, w_ffn2_down, g_ffn2_post):
    depth = state_mlstm_C.shape[0]
    assert depth == 1, "kernel handles a single layer"
    bp, tp, _ = x_prompt.shape
    bs, ts, _ = x_sample.shape
    l = 0

    w_in_t = jnp.swapaxes(w_in, 1, 2)
    w_gate_t = jnp.pad(w_in_t[l, D_MAIN:], ((0, LANES - 2 * N_HEADS), (0, 0))).astype(BF16)
    w_o = w_out[l].astype(BF16)
    bias = jnp.pad(jnp.concatenate([b_igate[l], b_fgate[l]]), (0, LANES - 2 * N_HEADS))[None, :]

    xs, *ffn1 = _ffn(x_sample.reshape(bs * ts, D_MODEL), g_ffn1_pre,
                     w_ffn1_gate[l], w_ffn1_up[l], w_ffn1_down[l], g_ffn1_post)
    proj_s, gates_s, w_main_t = _proj_in(xs, g_mix_pre, w_in_t, w_gate_t)
    m_rows = jnp.repeat(state_mlstm_m[l].T, ts, axis=1)[:, :, None]
    hm_s, c_s, n_s, m_s = _mlstm_sample(
        proj_s, gates_s, bias, m_rows, state_mlstm_C, state_mlstm_n[l][:, :, None, :], ts)
    glu_t = jnp.swapaxes(proj_s[:, 4 * D_MLSTM:].reshape(bs, ts, 2, D_CONV), 0, 1)
    hc_t, conv_t = _conv_sample(glu_t[:, :, 0], glu_t[:, :, 1], jnp.swapaxes(state_conv, 1, 2),
                                w_dw[l], b_dw, g_conv_ln, b_conv_ln)
    hc_s = jnp.swapaxes(hc_t, 0, 1).reshape(bs * ts, D_CONV)
    xs = _proj_out(hm_s, hc_s, xs, w_o, g_mix_post)
    ys, *ffn2 = _ffn(xs, g_ffn2_pre, w_ffn2_gate[l], w_ffn2_up[l], w_ffn2_down[l], g_ffn2_post)
    ys = ys.reshape(bs, ts, D_MODEL)

    xp = _ffn(x_prompt.reshape(bp * tp, D_MODEL), g_ffn1_pre, *ffn1, g_ffn1_post)[0]
    proj_p, gates_p = _proj_in(xp, g_mix_pre, w_main_t, w_gate_t)
    hc_p, conv_p = _conv_prompt(proj_p, w_dw[l], b_dw, g_conv_ln, b_conv_ln, bp, tp)
    xp, c_p, n_p, m_p = _mix_prompt(proj_p, gates_p, bias, hc_p, xp, w_o, g_mix_post, bp, tp)
    yp = _ffn(xp, g_ffn2_pre, *ffn2, g_ffn2_post)[0].reshape(bp, tp, D_MODEL)

    return (yp, ys,
            c_p[None], n_p[:, :, 0, :][None], m_p[:, :, 0, 0][None], conv_p,
            c_s, n_s[:, :, 0, :][None], m_s[:, ::ts, 0].T[None], jnp.swapaxes(conv_t, 1, 2))
```

```python
import functools

import jax
import jax.numpy as jnp
from jax import lax
from jax.experimental import pallas as pl
from jax.experimental.pallas import tpu as pltpu

D_MODEL = 2048
N_HEADS = 4
HEAD_DIM = 256
D_MLSTM = N_HEADS * HEAD_DIM
D_CONV = D_MODEL - D_MLSTM
CONV_WIDTH = 31
HALO = CONV_WIDTH - 1
CONV_PAD = 32
CONV_ROWS = 32
D_FF = 5632
FFN_TF = 512
D_MAIN = 4 * D_MLSTM + 2 * D_CONV
EPS = 1e-6
FFN_RES = 0.5
K_SCALE = HEAD_DIM ** -0.5

LANES = 128
ROWS = 128
VMEM_LIMIT = 56 * 1024 * 1024

F32 = jnp.float32
BF16 = jnp.bfloat16


def _params(sem):
    return pltpu.CompilerParams(dimension_semantics=sem, vmem_limit_bytes=VMEM_LIMIT)


def _rms(x, g):
    return x * lax.rsqrt(jnp.mean(x * x, axis=-1, keepdims=True) + EPS) * g


def _ffn_cast_body(x_ref, gpre_ref, wg_ref, wu_ref, wd_ref, gpost_ref, o_ref, wgu_o, wd_o,
                   h_ref):
    f = pl.program_id(1)

    @pl.when(f == 0)
    def _():
        h_ref[...] = _rms(x_ref[...], gpre_ref[...]).astype(BF16)
        o_ref[...] = jnp.zeros_like(o_ref)

    tf = wg_ref.shape[1]
    wgu_o[0, :, :tf] = wg_ref[...].astype(BF16)
    wgu_o[0, :, tf:] = wu_ref[...].astype(BF16)
    wd_o[...] = wd_ref[...].astype(BF16)
    o_ref[...] += jnp.dot(_swiglu_act(h_ref[...], wgu_o[0]), wd_o[...], preferred_element_type=F32)

    @pl.when(f == pl.num_programs(1) - 1)
    def _():
        o_ref[...] = x_ref[...] + FFN_RES * _rms(o_ref[...], gpost_ref[...])


def _swiglu_act(h, w_gate_up):
    gu = jnp.dot(h, w_gate_up, preferred_element_type=F32)
    tf = gu.shape[1] // 2
    g, u = gu[:, :tf], gu[:, tf:]
    return ((g * jax.nn.sigmoid(g)) * u).astype(BF16)


def _ffn_skew_body(x_ref, gpre_ref, wgu_ref, wd_ref, gpost_ref, o_ref,
                   h_ref, a0_ref, a1_ref, *, nf):
    f = pl.program_id(1)

    def gate_up(a_ref):
        a_ref[...] = _swiglu_act(h_ref[...], wgu_ref[0])

    def down(a_ref):
        o_ref[...] += jnp.dot(a_ref[...], wd_ref[...], preferred_element_type=F32)

    @pl.when(f == 0)
    def _():
        h_ref[...] = _rms(x_ref[...], gpre_ref[...]).astype(BF16)
        o_ref[...] = jnp.zeros_like(o_ref)
        gate_up(a0_ref)

    @pl.when((f > 0) & (f < nf) & ((f & 1) == 1))
    def _():
        gate_up(a1_ref)
        down(a0_ref)

    @pl.when((f > 0) & (f < nf) & ((f & 1) == 0))
    def _():
        gate_up(a0_ref)
        down(a1_ref)

    @pl.when(f == nf)
    def _():
        down(a0_ref if (nf - 1) % 2 == 0 else a1_ref)
        o_ref[...] = x_ref[...] + FFN_RES * _rms(o_ref[...], gpost_ref[...])


def _ffn(x, gpre, w_up, wd, gpost):
    m = x.shape[0]
    cast = wd.dtype == F32
    tm = min(m, 512 if cast else 1024)
    nf = D_FF // FFN_TF
    row = lambda i, f: (i, 0)
    const = lambda i, f: (0, 0)
    out_specs = [pl.BlockSpec((tm, D_MODEL), row)]
    out_shape = [jax.ShapeDtypeStruct((m, D_MODEL), F32)]
    if cast:
        assert m == tm, "the casting variant writes each weight tile once"
        tf = FFN_TF
        body = _ffn_cast_body
        steps = nf
        up_spec = pl.BlockSpec((D_MODEL, tf), lambda i, f: (0, f))
        up_specs = [up_spec, up_spec]
        up_args = list(w_up)
        down_spec = pl.BlockSpec((tf, D_MODEL), lambda i, f: (f, 0))
        out_specs += [pl.BlockSpec((1, D_MODEL, 2 * tf), lambda i, f: (f, 0, 0)), down_spec]
        out_shape += [jax.ShapeDtypeStruct((nf, D_MODEL, 2 * tf), BF16), jax.ShapeDtypeStruct(wd.shape, BF16)]
        scratch = []
    else:
        tf = FFN_TF
        body = functools.partial(_ffn_skew_body, nf=nf)
        steps = nf + 1
        up_specs = [pl.BlockSpec((1, D_MODEL, 2 * tf), lambda i, f: (jnp.minimum(f, nf - 1), 0, 0))]
        up_args = [w_up]
        down_spec = pl.BlockSpec((tf, D_MODEL), lambda i, f: (jnp.maximum(f - 1, 0), 0))
        scratch = [pltpu.VMEM((tm, tf), BF16), pltpu.VMEM((tm, tf), BF16)]
    return pl.pallas_call(
        body,
        grid=(m // tm, steps),
        in_specs=[pl.BlockSpec((tm, D_MODEL), row), pl.BlockSpec((1, D_MODEL), const),
                  *up_specs, down_spec, pl.BlockSpec((1, D_MODEL), const)],
        out_specs=out_specs,
        out_shape=out_shape,
        scratch_shapes=[pltpu.VMEM((tm, D_MODEL), BF16)] + scratch,
        compiler_params=_params(("parallel", "arbitrary")),
        name="ffn_cast" if cast else "ffn",
    )(x, gpre, *up_args, wd, gpost)


def _ffn_stream_body(x_ref, gpre_ref, wgu_hbm, wd_hbm, gpost_ref, o_ref,
                     h_ref, a_ref, wgu_buf, wd_buf, sem, *, nf):
    i = pl.program_id(0)
    tf = wd_buf.shape[1]

    def gu_copy(c):
        return pltpu.make_async_copy(wgu_hbm.at[c], wgu_buf.at[c % 2], sem.at[0, c % 2])

    def d_copy(c):
        return pltpu.make_async_copy(wd_hbm.at[pl.ds(c * tf, tf)], wd_buf.at[c % 2], sem.at[1, c % 2])

    @pl.when(i == 0)
    def _():
        gu_copy(0).start()

    h_ref[...] = _rms(x_ref[...], gpre_ref[...]).astype(BF16)
    o_ref[...] = jnp.zeros_like(o_ref)
    for c in range(nf + 1):
        if c < nf:
            gu_copy(c).wait()
        if c >= 1:
            d_copy(c - 1).wait()
        if c + 1 < nf:
            gu_copy(c + 1).start()
        if c < nf:
            d_copy(c).start()
            a_ref[c % 2] = _swiglu_act(h_ref[...], wgu_buf[c % 2])
        if c >= 1:
            o_ref[...] += jnp.dot(a_ref[(c - 1) % 2], wd_buf[(c - 1) % 2], preferred_element_type=F32)
    o_ref[...] = x_ref[...] + FFN_RES * _rms(o_ref[...], gpost_ref[...])

    @pl.when(i + 1 < pl.num_programs(0))
    def _():
        gu_copy(0).start()


def _ffn_stream(x, gpre, w_gu, wd, gpost, *, tm=1024):
    m = x.shape[0]
    nf, _, tf2 = w_gu.shape
    tf = tf2 // 2
    row = lambda i: (i, 0)
    const = lambda i: (0, 0)
    return pl.pallas_call(
        functools.partial(_ffn_stream_body, nf=nf),
        grid=(m // tm,),
        in_specs=[pl.BlockSpec((tm, D_MODEL), row), pl.BlockSpec((1, D_MODEL), const),
                  pl.BlockSpec(memory_space=pl.ANY), pl.BlockSpec(memory_space=pl.ANY),
                  pl.BlockSpec((1, D_MODEL), const)],
        out_specs=pl.BlockSpec((tm, D_MODEL), row),
        out_shape=jax.ShapeDtypeStruct((m, D_MODEL), F32),
        scratch_shapes=[pltpu.VMEM((tm, D_MODEL), BF16), pltpu.VMEM((2, tm, tf), BF16),
                        pltpu.VMEM((2, D_MODEL, tf2), BF16), pltpu.VMEM((2, tf, D_MODEL), BF16),
                        pltpu.SemaphoreType.DMA((2, 2))],
        compiler_params=_params(("arbitrary",)),
        name="ffn_stream",
    )(x, gpre, w_gu, wd, gpost)


def _proj_in_body(*refs, cast):
    if cast:
        x_ref, g_ref, w_ref, wgate_ref, proj_ref, gates_ref, w_o, h_ref = refs
    else:
        x_ref, g_ref, w_ref, wgate_ref, proj_ref, gates_ref, h_ref = refs
    nt = (((1,), (1,)), ((), ()))

    @pl.when(pl.program_id(1) == 0)
    def _():
        h = _rms(x_ref[...], g_ref[...]).astype(BF16)
        h_ref[...] = h
        gates_ref[...] = lax.dot_general(h, wgate_ref[...], nt, preferred_element_type=F32)

    if cast:
        w_o[...] = w_ref[0].astype(BF16)
        w_ref = w_o
    proj_ref[...] = lax.dot_general(h_ref[...], w_ref[...], nt, preferred_element_type=F32)


def _proj_in(x, g, w_main_t, w_gate_t):
    m = x.shape[0]
    cast = w_main_t.dtype == F32
    tm = min(m, 1024)
    tn = 1024 if cast else 1536
    if cast:
        assert m == tm, "the casting variant writes each weight tile once"
        w_spec = pl.BlockSpec((1, tn, D_MODEL), lambda i, n: (0, n, 0))
    else:
        w_spec = pl.BlockSpec((tn, D_MODEL), lambda i, n: (n, 0))
    out_specs = [pl.BlockSpec((tm, tn), lambda i, n: (i, n)), pl.BlockSpec((tm, LANES), lambda i, n: (i, 0))]
    out_shape = [jax.ShapeDtypeStruct((m, D_MAIN), F32), jax.ShapeDtypeStruct((m, LANES), F32)]
    if cast:
        out_specs.append(pl.BlockSpec((tn, D_MODEL), lambda i, n: (n, 0)))
        out_shape.append(jax.ShapeDtypeStruct((D_MAIN, D_MODEL), BF16))
    return pl.pallas_call(
        functools.partial(_proj_in_body, cast=cast),
        grid=(m // tm, D_MAIN // tn),
        in_specs=[
            pl.BlockSpec((tm, D_MODEL), lambda i, n: (i, 0)),
            pl.BlockSpec((1, D_MODEL), lambda i, n: (0, 0)),
            w_spec,
            pl.BlockSpec((LANES, D_MODEL), lambda i, n: (0, 0)),
        ],
        out_specs=out_specs,
        out_shape=out_shape,
        scratch_shapes=[pltpu.VMEM((tm, D_MODEL), BF16)],
        compiler_params=_params(("parallel", "arbitrary")),
        name="proj_in_cast" if cast else "proj_in",
    )(x, g, w_main_t, w_gate_t)


def _log_sigmoid(x):
    return jnp.minimum(x, 0.0) - jnp.log1p(jnp.exp(-jnp.abs(x)))


def _seg_cumsum(x, seg_len):
    pos = lax.broadcasted_iota(jnp.int32, x.shape, 0) & (seg_len - 1)
    shift = 1
    while shift < seg_len:
        x = x + jnp.where(pos >= shift, pltpu.roll(x, shift, 0), 0.0)
        shift *= 2
    return x


def _gate_terms(pre, seg_len):
    bt = _seg_cumsum(_log_sigmoid(pre), seg_len)
    return pre, bt, pre.T, bt.T


def _tile_masks(seg_len):
    log2 = seg_len.bit_length() - 1
    t_idx = lax.broadcasted_iota(jnp.int32, (ROWS, ROWS), 0)
    s_idx = lax.broadcasted_iota(jnp.int32, (ROWS, ROWS), 1)
    if seg_len == ROWS:
        return s_idx <= t_idx, None, None
    same = (t_idx >> log2) == (s_idx >> log2)
    last = s_idx == ((t_idx >> log2) << log2) + (seg_len - 1)
    return same & (s_idx <= t_idx), same, last


def _pick(x, j, axis):
    if isinstance(j, int):
        return x[:, j:j + 1] if axis == 1 else x[j:j + 1, :]
    idx = lax.broadcasted_iota(jnp.int32, x.shape, axis)
    return jnp.sum(jnp.where(idx == j, x, 0.0), axis=axis, keepdims=True)


def _mlstm_tile(q, k, v, gates, masks, hd, seg_len, m_prev):
    pre, bt_all, pre_t, bt_t = gates
    valid, same, last = masks
    ig_col = _pick(pre, hd, 1)
    bt_col = _pick(bt_all, hd + N_HEADS, 1)
    key_w = _pick(pre_t, hd, 0) - _pick(bt_t, hd + N_HEADS, 0)

    d = jnp.where(valid, bt_col + key_w, -jnp.inf)
    inter = bt_col + m_prev
    m_t = jnp.maximum(inter, jnp.max(d, axis=1, keepdims=True))
    w_intra = jnp.exp(d - m_t)
    w_inter = jnp.exp(inter - m_t)

    if seg_len == ROWS:
        bt_last = bt_col[ROWS - 1:ROWS, :]
        m_end = m_t[ROWS - 1:ROWS, :]
    else:
        bt_row = _pick(bt_t, hd + N_HEADS, 0)
        bt_last = jnp.sum(jnp.where(last, bt_row, 0.0), axis=1, keepdims=True)
        e = jnp.where(same, bt_last + key_w, -jnp.inf)
        m_end = jnp.maximum(bt_last + m_prev, jnp.max(e, axis=1, keepdims=True))
    g_keys = jnp.exp(bt_last - bt_col + ig_col - m_end)
    g_state = jnp.exp(bt_last + m_prev - m_end)

    ks = k * K_SCALE
    q_bf = q.astype(BF16)
    v_bf = v.astype(BF16)
    s = lax.dot_general(q_bf, ks.astype(BF16), (((1,), (1,)), ((), ())),
                        preferred_element_type=F32) * w_intra
    num = jnp.dot(s.astype(BF16), v_bf, preferred_element_type=F32)
    den = jnp.sum(s, axis=1, keepdims=True)
    kg = ks * g_keys
    return dict(q_bf=q_bf, v_bf=v_bf, num=num, den=den, kg=kg, m_t=m_t, w_inter=w_inter,
                g_state=g_state, m_end=m_end)


def _mlstm_out(t, q, o, q_c, q_n):
    num = t["num"] + q_c * t["w_inter"]
    den = t["den"] + q_n * t["w_inter"]
    h = num / jnp.maximum(jnp.abs(den), jnp.exp(-t["m_t"]))
    return (jax.nn.sigmoid(o) * h).astype(BF16)


def _mix_prompt_body(q_ref, k_ref, v_ref, o_ref, gates_ref, bias_ref, hc_ref, x_ref, wa_ref, wb_ref, g_ref,
                     y_ref, c_out_ref, n_out_ref, m_out_ref, c_ref, n_ref, m_ref, hm_ref, *, n_chunks, nt):
    t = pl.program_id(1)

    def project():
        mix = (jnp.dot(hm_ref[(t - 1) & 1], wa_ref[...], preferred_element_type=F32)
               + jnp.dot(hc_ref[...], wb_ref[...], preferred_element_type=F32))
        y_ref[...] = x_ref[...] + _rms(mix, g_ref[...])

    def recur():
        slot = t & 1
        masks = _tile_masks(ROWS)
        for c in range(n_chunks):
            rows = pl.ds(c * ROWS, ROWS)
            gates = _gate_terms(gates_ref[rows, :] + bias_ref[...], ROWS)
            for hd in range(N_HEADS):
                cols = pl.ds(hd * HEAD_DIM, HEAD_DIM)
                q = q_ref[rows, cols]
                tl = _mlstm_tile(q, k_ref[rows, cols], v_ref[rows, cols], gates, masks, hd, ROWS,
                                 m_ref[hd][:, 0:1])
                c_old = c_ref[hd]
                n_old = n_ref[hd]
                q_c = jnp.dot(tl["q_bf"], c_old.astype(BF16), preferred_element_type=F32)
                q_n = jnp.sum(q * n_old, axis=1, keepdims=True)
                hm_ref[slot, rows, cols] = _mlstm_out(tl, q, o_ref[rows, cols], q_c, q_n)
                g = tl["g_state"]
                c_ref[hd] = g * c_old + lax.dot_general(
                    tl["kg"].astype(BF16), tl["v_bf"], (((0,), (0,)), ((), ())), preferred_element_type=F32)
                n_ref[hd] = g * n_old + jnp.sum(tl["kg"], axis=0, keepdims=True)
                m_ref[hd] = jnp.broadcast_to(tl["m_end"], (1, LANES))

    @pl.when(t == 0)
    def _():
        c_ref[...] = jnp.zeros_like(c_ref)
        n_ref[...] = jnp.zeros_like(n_ref)
        m_ref[...] = jnp.zeros_like(m_ref)
        recur()

    @pl.when((t > 0) & (t < nt))
    def _():
        project()
        recur()

    @pl.when(t == nt)
    def _():
        project()
        c_out_ref[0] = c_ref[...]
        n_out_ref[0] = n_ref[...]
        m_out_ref[0] = m_ref[...]


def _mix_prompt(proj, gates, bias, hc, x, w_out, g, batch, seq, *, tt=256):
    nt = seq // tt
    cur = lambda j: (lambda b, t: (b * nt + jnp.minimum(t, nt - 1), j))
    prev = lambda b, t: (b * nt + jnp.maximum(t - 1, 0), 0)
    state = lambda b, t: (b, 0, 0, 0)
    const = lambda b, t: (0, 0)
    return pl.pallas_call(
        functools.partial(_mix_prompt_body, n_chunks=tt // ROWS, nt=nt),
        grid=(batch, nt + 1),
        in_specs=[
            pl.BlockSpec((tt, D_MLSTM), cur(0)),
            pl.BlockSpec((tt, D_MLSTM), cur(1)),
            pl.BlockSpec((tt, D_MLSTM), cur(2)),
            pl.BlockSpec((tt, D_MLSTM), cur(3)),
            pl.BlockSpec((tt, LANES), cur(0)),
            pl.BlockSpec((1, LANES), const),
            pl.BlockSpec((tt, D_CONV), prev),
            pl.BlockSpec((tt, D_MODEL), prev),
            pl.BlockSpec((D_MLSTM, D_MODEL), lambda b, t: (0, 0)),
            pl.BlockSpec((D_CONV, D_MODEL), lambda b, t: (1, 0)),
            pl.BlockSpec((1, D_MODEL), const),
        ],
        out_specs=[
            pl.BlockSpec((tt, D_MODEL), prev),
            pl.BlockSpec((1, N_HEADS, HEAD_DIM, HEAD_DIM), state),
            pl.BlockSpec((1, N_HEADS, 1, HEAD_DIM), state),
            pl.BlockSpec((1, N_HEADS, 1, LANES), state),
        ],
        out_shape=[
            jax.ShapeDtypeStruct((batch * seq, D_MODEL), F32),
            jax.ShapeDtypeStruct((batch, N_HEADS, HEAD_DIM, HEAD_DIM), F32),
            jax.ShapeDtypeStruct((batch, N_HEADS, 1, HEAD_DIM), F32),
            jax.ShapeDtypeStruct((batch, N_HEADS, 1, LANES), F32),
        ],
        scratch_shapes=[pltpu.VMEM((N_HEADS, HEAD_DIM, HEAD_DIM), F32), pltpu.VMEM((N_HEADS, 1, HEAD_DIM), F32),
                        pltpu.VMEM((N_HEADS, 1, LANES), F32), pltpu.VMEM((2, tt, D_MLSTM), BF16)],
        compiler_params=_params(("parallel", "arbitrary")),
        name="mix_prompt",
    )(proj, proj, proj, proj, gates, bias, hc, x, w_out, w_out, g)


def _mlstm_sample_body(q_ref, k_ref, v_ref, o_ref, gates_ref, bias_ref, mrow_ref, c_ref, n_ref,
                       hm_ref, c_out_ref, n_out_ref, m_out_ref, *, seg_len):
    hd = pl.program_id(1)
    n_seg = ROWS // seg_len
    grp = 16 // seg_len
    log2 = seg_len.bit_length() - 1
    q = q_ref[...]
    gates = _gate_terms(gates_ref[...] + bias_ref[...], seg_len)
    t = _mlstm_tile(q, k_ref[...], v_ref[...], gates, _tile_masks(seg_len), hd, seg_len, mrow_ref[0])

    seg_of_row = lax.broadcasted_iota(jnp.int32, (16, 1), 0) >> log2
    qc_parts, n_parts = [], []
    for j in range(ROWS // 16):
        qg = t["q_bf"][16 * j:16 * (j + 1)]
        qc, nr = None, None
        for i in range(grp):
            b = grp * j + i
            r = jnp.dot(qg, c_ref[0, b, 0].astype(BF16), preferred_element_type=F32)
            nb = jnp.broadcast_to(n_ref[b, 0], (16, HEAD_DIM))
            qc = r if i == 0 else jnp.where(seg_of_row == i, r, qc)
            nr = nb if i == 0 else jnp.where(seg_of_row == i, nb, nr)
        qc_parts.append(qc)
        n_parts.append(nr)
    q_c = jnp.concatenate(qc_parts, axis=0)
    q_n = jnp.sum(q * jnp.concatenate(n_parts, axis=0), axis=1, keepdims=True)
    hm_ref[...] = _mlstm_out(t, q, o_ref[...], q_c, q_n)

    kg = t["kg"]
    kg_t = kg.T
    seg_of_lane = lax.broadcasted_iota(jnp.int32, (1, ROWS), 1) >> log2
    seg_of_row8 = lax.broadcasted_iota(jnp.int32, (8, 1), 0) >> log2
    per8 = 8 // seg_len
    for b in range(n_seg):
        g = t["g_state"][seg_len * b:seg_len * b + 1, :]
        upd = jnp.dot(jnp.where(seg_of_lane == b, kg_t, 0.0).astype(BF16), t["v_bf"],
                      preferred_element_type=F32)
        c_out_ref[0, b, 0] = g * c_ref[0, b, 0] + upd
        kg8 = kg[8 * (b // per8):8 * (b // per8) + 8]
        n_out_ref[b, 0] = g * n_ref[b, 0] + jnp.sum(
            jnp.where(seg_of_row8 == (b % per8), kg8, 0.0), axis=0, keepdims=True)
    m_out_ref[0] = t["m_end"]


def _mlstm_sample(proj, gates, bias, m_rows, c0, n0, seg_len):
    m = proj.shape[0]
    n_seg = ROWS // seg_len
    col = lambda j: (lambda i, h: (i, j * N_HEADS + h))
    return pl.pallas_call(
        functools.partial(_mlstm_sample_body, seg_len=seg_len),
        grid=(m // ROWS, N_HEADS),
        in_specs=[
            pl.BlockSpec((ROWS, HEAD_DIM), col(0)),
            pl.BlockSpec((ROWS, HEAD_DIM), col(1)),
            pl.BlockSpec((ROWS, HEAD_DIM), col(2)),
            pl.BlockSpec((ROWS, HEAD_DIM), col(3)),
            pl.BlockSpec((ROWS, LANES), lambda i, h: (i, 0)),
            pl.BlockSpec((1, LANES), lambda i, h: (0, 0)),
            pl.BlockSpec((1, ROWS, 1), lambda i, h: (h, i, 0)),
            pl.BlockSpec((1, n_seg, 1, HEAD_DIM, HEAD_DIM), lambda i, h: (0, i, h, 0, 0)),
            pl.BlockSpec((n_seg, 1, 1, HEAD_DIM), lambda i, h: (i, h, 0, 0)),
        ],
        out_specs=[
            pl.BlockSpec((ROWS, HEAD_DIM), lambda i, h: (i, h)),
            pl.BlockSpec((1, n_seg, 1, HEAD_DIM, HEAD_DIM), lambda i, h: (0, i, h, 0, 0)),
            pl.BlockSpec((n_seg, 1, 1, HEAD_DIM), lambda i, h: (i, h, 0, 0)),
            pl.BlockSpec((1, ROWS, 1), lambda i, h: (h, i, 0)),
        ],
        out_shape=[
            jax.ShapeDtypeStruct((m, D_MLSTM), BF16),
            jax.ShapeDtypeStruct(c0.shape, F32),
            jax.ShapeDtypeStruct(n0.shape, F32),
            jax.ShapeDtypeStruct(m_rows.shape, F32),
        ],
        compiler_params=_params(("parallel", "parallel")),
        name="mlstm_sample",
    )(proj, proj, proj, proj, gates, bias, m_rows, c0, n0)


def _ln_swish(y, g, b):
    mu = jnp.mean(y, axis=-1, keepdims=True)
    yc = y - mu
    var = jnp.mean(yc * yc, axis=-1, keepdims=True)
    z = yc * lax.rsqrt(var + EPS) * g + b
    return z * jax.nn.sigmoid(z)


def _conv_prompt_body(a_ref, b_ref, w_ref, bdw_ref, gln_ref, bln_ref, hc_ref, st_ref,
                      u_ref, us_ref, wb_ref, y_ref, *, tt):
    t_id = pl.program_id(1)
    n_shift = tt + CONV_PAD - 8

    @pl.when(t_id == 0)
    def _():
        u_ref[0:CONV_PAD, :] = jnp.zeros((CONV_PAD, D_CONV), F32)
        for s in range(CONV_WIDTH):
            wb_ref[s] = jnp.broadcast_to(w_ref[s:s + 1, :], (8, D_CONV))

    u_ref[CONV_PAD:CONV_PAD + tt, :] = a_ref[...] * jax.nn.sigmoid(b_ref[...])
    for r in range(1, 8):
        us_ref[r - 1] = u_ref[r:r + n_shift, :]

    def row_block(i, carry):
        base = pl.multiple_of(i * CONV_ROWS, CONV_ROWS)
        n_slab = CONV_ROWS // 8
        acc = [jnp.broadcast_to(bdw_ref[...], (8, D_CONV))] * n_slab
        for s in range(CONV_WIDTH):
            k8, r = divmod(CONV_PAD - HALO + s, 8)
            w = wb_ref[s]
            for j in range(n_slab):
                rows = pl.ds(pl.multiple_of(base + 8 * (k8 + j), 8), 8)
                win = u_ref[rows, :] if r == 0 else us_ref[r - 1, rows, :]
                acc[j] = acc[j] + w * win
        for j in range(n_slab):
            y_ref[pl.ds(pl.multiple_of(base + 8 * j, 8), 8), :] = acc[j]
        return carry

    lax.fori_loop(0, tt // CONV_ROWS, row_block, 0)
    hc_ref[...] = _ln_swish(y_ref[...], gln_ref[...], bln_ref[...]).astype(BF16)

    @pl.when(t_id == pl.num_programs(1) - 1)
    def _():
        st_ref[0, 0] = u_ref[CONV_PAD + tt - HALO:CONV_PAD + tt, :]

    u_ref[0:CONV_PAD, :] = u_ref[tt:tt + CONV_PAD, :]


def _conv_prompt(proj, w_dw, b_dw, g_ln, b_ln, batch, seq, *, tt=256):
    nt = seq // tt
    const = lambda b, t: (0, 0)
    return pl.pallas_call(
        functools.partial(_conv_prompt_body, tt=tt),
        grid=(batch, nt),
        in_specs=[
            pl.BlockSpec((tt, D_CONV), lambda b, t: (b * nt + t, 4)),
            pl.BlockSpec((tt, D_CONV), lambda b, t: (b * nt + t, 5)),
            pl.BlockSpec((CONV_WIDTH, D_CONV), const),
            pl.BlockSpec((1, D_CONV), const),
            pl.BlockSpec((1, D_CONV), const),
            pl.BlockSpec((1, D_CONV), const),
        ],
        out_specs=[
            pl.BlockSpec((tt, D_CONV), lambda b, t: (b * nt + t, 0)),
            pl.BlockSpec((1, 1, HALO, D_CONV), lambda b, t: (0, b, 0, 0)),
        ],
        out_shape=[
            jax.ShapeDtypeStruct((batch * seq, D_CONV), BF16),
            jax.ShapeDtypeStruct((1, batch, HALO, D_CONV), F32),
        ],
        scratch_shapes=[pltpu.VMEM((tt + CONV_PAD, D_CONV), F32),
                        pltpu.VMEM((7, tt + CONV_PAD - 8, D_CONV), F32),
                        pltpu.VMEM((CONV_WIDTH, 8, D_CONV), F32),
                        pltpu.VMEM((tt, D_CONV), F32)],
        compiler_params=_params(("parallel", "arbitrary")),
        name="conv_prompt",
    )(proj, proj, w_dw, b_dw, g_ln, b_ln)


def _conv_sample_body(a_ref, b_ref, st_ref, w_ref, bdw_ref, gln_ref, bln_ref, hc_ref, st_out_ref,
                      wb_ref, *, seq):
    @pl.when(pl.program_id(0) == 0)
    def _():
        for s in range(CONV_WIDTH):
            wb_ref[s] = jnp.broadcast_to(w_ref[s:s + 1, :], (8, D_CONV))

    acc = [jnp.broadcast_to(bdw_ref[...], (8, D_CONV))] * seq
    for j in range(HALO + seq):
        slab = st_ref[0, j] if j < HALO else a_ref[j - HALO] * jax.nn.sigmoid(b_ref[j - HALO])
        for t in range(seq):
            if 0 <= j - t < CONV_WIDTH:
                acc[t] = acc[t] + wb_ref[j - t] * slab
        if j >= seq:
            st_out_ref[0, j - seq] = slab
    for t in range(seq):
        hc_ref[t] = _ln_swish(acc[t], gln_ref[...], bln_ref[...])


def _conv_sample(a_t, b_t, state_t, w_dw, b_dw, g_ln, b_ln):
    seq, batch, _ = a_t.shape
    bb = 8
    const = lambda i: (0, 0)
    tok = pl.BlockSpec((seq, bb, D_CONV), lambda i: (0, i, 0))
    hist = pl.BlockSpec((1, HALO, bb, D_CONV), lambda i: (0, 0, i, 0))
    return pl.pallas_call(
        functools.partial(_conv_sample_body, seq=seq),
        grid=(batch // bb,),
        in_specs=[tok, tok, hist,
                  pl.BlockSpec((CONV_WIDTH, D_CONV), const),
                  pl.BlockSpec((1, D_CONV), const),
                  pl.BlockSpec((1, D_CONV), const),
                  pl.BlockSpec((1, D_CONV), const)],
        out_specs=[tok, hist],
        out_shape=[
            jax.ShapeDtypeStruct(a_t.shape, F32),
            jax.ShapeDtypeStruct(state_t.shape, F32),
        ],
        scratch_shapes=[pltpu.VMEM((CONV_WIDTH, 8, D_CONV), F32)],
        compiler_params=_params(("arbitrary",)),
        name="conv_sample",
    )(a_t, b_t, state_t, w_dw, b_dw, g_ln, b_ln)


def _proj_out_body(hm_ref, hc_ref, x_ref, wa_ref, wb_ref, g_ref, o_ref):
    mix = (jnp.dot(hm_ref[...], wa_ref[...], preferred_element_type=F32)
           + jnp.dot(hc_ref[...].astype(BF16), wb_ref[...], preferred_element_type=F32))
    o_ref[...] = x_ref[...] + _rms(mix, g_ref[...])


def _proj_out(hm, hc, x, w_out, g, *, tm=512):
    m = x.shape[0]
    row = lambda i: (i, 0)
    return pl.pallas_call(
        _proj_out_body,
        grid=(m // tm,),
        in_specs=[
            pl.BlockSpec((tm, D_MLSTM), row),
            pl.BlockSpec((tm, D_CONV), row),
            pl.BlockSpec((tm, D_MODEL), row),
            pl.BlockSpec((D_MLSTM, D_MODEL), lambda i: (0, 0)),
            pl.BlockSpec((D_CONV, D_MODEL), lambda i: (1, 0)),
            pl.BlockSpec((1, D_MODEL), lambda i: (0, 0)),
        ],
        out_specs=pl.BlockSpec((tm, D_MODEL), row),
        out_shape=jax.ShapeDtypeStruct((m, D_MODEL), F32),
        compiler_params=_params(("parallel",)),
        name="proj_out",
    )(hm, hc, x, w_out, w_out, g)


def kernel(x_prompt, x_sample, state_mlstm_C, state_mlstm_n, state_mlstm_m, state_conv, g_ffn1_pre, w_ffn1_gate, w_ffn1_up, w_ffn1_down, g_ffn1_post, g_mix_pre, w_in, b_igate, b_fgate, w_dw, b_dw, g_conv_ln, b_conv_ln, w_out, g_mix_post, g_ffn2_pre, w_ffn2_gate, w_ffn2_up, w_ffn2_down, g_ffn2_post):
    depth = state_mlstm_C.shape[0]
    assert depth == 1, "kernel handles a single layer"
    bp, tp, _ = x_prompt.shape
    bs, ts, _ = x_sample.shape
    l = 0

    w_in_t = jnp.swapaxes(w_in, 1, 2)
    w_gate_t = jnp.pad(w_in_t[l, D_MAIN:], ((0, LANES - 2 * N_HEADS), (0, 0))).astype(BF16)
    w_o = w_out[l].astype(BF16)
    bias = jnp.pad(jnp.concatenate([b_igate[l], b_fgate[l]]), (0, LANES - 2 * N_HEADS))[None, :]

    xs, *ffn1 = _ffn(x_sample.reshape(bs * ts, D_MODEL), g_ffn1_pre,
                     (w_ffn1_gate[l], w_ffn1_up[l]), w_ffn1_down[l], g_ffn1_post)
    proj_s, gates_s, w_main_t = _proj_in(xs, g_mix_pre, w_in_t, w_gate_t)
    m_rows = jnp.repeat(state_mlstm_m[l].T, ts, axis=1)[:, :, None]
    hm_s, c_s, n_s, m_s = _mlstm_sample(
        proj_s, gates_s, bias, m_rows, state_mlstm_C, state_mlstm_n[l][:, :, None, :], ts)
    glu_t = jnp.swapaxes(proj_s[:, 4 * D_MLSTM:].reshape(bs, ts, 2, D_CONV), 0, 1)
    hc_t, conv_t = _conv_sample(glu_t[:, :, 0], glu_t[:, :, 1], jnp.swapaxes(state_conv, 1, 2),
                                w_dw[l], b_dw, g_conv_ln, b_conv_ln)
    hc_s = jnp.swapaxes(hc_t, 0, 1).reshape(bs * ts, D_CONV)
    xs = _proj_out(hm_s, hc_s, xs, w_o, g_mix_post)
    ys, *ffn2 = _ffn(xs, g_ffn2_pre, (w_ffn2_gate[l], w_ffn2_up[l]), w_ffn2_down[l], g_ffn2_post)
    ys = ys.reshape(bs, ts, D_MODEL)

    xp = _ffn_stream(x_prompt.reshape(bp * tp, D_MODEL), g_ffn1_pre, *ffn1, g_ffn1_post)
    proj_p, gates_p = _proj_in(xp, g_mix_pre, w_main_t, w_gate_t)
    hc_p, conv_p = _conv_prompt(proj_p, w_dw[l], b_dw, g_conv_ln, b_conv_ln, bp, tp)
    xp, c_p, n_p, m_p = _mix_prompt(proj_p, gates_p, bias, hc_p, xp, w_o, g_mix_post, bp, tp)
    yp = _ffn_stream(xp, g_ffn2_pre, *ffn2, g_ffn2_post).reshape(bp, tp, D_MODEL)

    return (yp, ys,
            c_p[None], n_p[:, :, 0, :][None], m_p[:, :, 0, 0][None], conv_p,
            c_s, n_s[:, :, 0, :][None], m_s[:, ::ts, 0].T[None], jnp.swapaxes(conv_t, 1, 2))
```

```python
import functools

import jax
import jax.numpy as jnp
from jax import lax
from jax.experimental import pallas as pl
from jax.experimental.pallas import tpu as pltpu

D_MODEL = 2048
N_HEADS = 4
HEAD_DIM = 256
D_MLSTM = N_HEADS * HEAD_DIM
D_CONV = D_MODEL - D_MLSTM
CONV_WIDTH = 31
HALO = CONV_WIDTH - 1
CONV_PAD = 32
CONV_ROWS = 32
D_FF = 5632
FFN_TF = 512
D_MAIN = 4 * D_MLSTM + 2 * D_CONV
EPS = 1e-6
FFN_RES = 0.5
K_SCALE = HEAD_DIM ** -0.5

LANES = 128
ROWS = 128
VMEM_LIMIT = 56 * 1024 * 1024

F32 = jnp.float32
BF16 = jnp.bfloat16


def _params(sem):
    return pltpu.CompilerParams(dimension_semantics=sem, vmem_limit_bytes=VMEM_LIMIT)


def _rms(x, g):
    return x * lax.rsqrt(jnp.mean(x * x, axis=-1, keepdims=True) + EPS) * g


def _ffn_cast_body(x_ref, gpre_ref, wg_ref, wu_ref, wd_ref, gpost_ref, o_ref, wgu_o, wd_o,
                   h_ref):
    f = pl.program_id(1)

    @pl.when(f == 0)
    def _():
        h_ref[...] = _rms(x_ref[...], gpre_ref[...]).astype(BF16)
        o_ref[...] = jnp.zeros_like(o_ref)

    tf = wg_ref.shape[1]
    wgu_o[0, :, :tf] = wg_ref[...].astype(BF16)
    wgu_o[0, :, tf:] = wu_ref[...].astype(BF16)
    wd_o[...] = wd_ref[...].astype(BF16)
    o_ref[...] += jnp.dot(_swiglu_act(h_ref[...], wgu_o[0]), wd_o[...], preferred_element_type=F32)

    @pl.when(f == pl.num_programs(1) - 1)
    def _():
        o_ref[...] = x_ref[...] + FFN_RES * _rms(o_ref[...], gpost_ref[...])


def _swiglu_act(h, w_gate_up):
    gu = jnp.dot(h, w_gate_up, preferred_element_type=F32)
    tf = gu.shape[1] // 2
    g, u = gu[:, :tf], gu[:, tf:]
    return ((g * jax.nn.sigmoid(g)) * u).astype(BF16)


def _ffn_skew_body(x_ref, gpre_ref, wgu_ref, wd_ref, gpost_ref, o_ref,
                   h_ref, a0_ref, a1_ref, *, nf):
    f = pl.program_id(1)

    def gate_up(a_ref):
        a_ref[...] = _swiglu_act(h_ref[...], wgu_ref[0])

    def down(a_ref):
        o_ref[...] += jnp.dot(a_ref[...], wd_ref[...], preferred_element_type=F32)

    @pl.when(f == 0)
    def _():
        h_ref[...] = _rms(x_ref[...], gpre_ref[...]).astype(BF16)
        o_ref[...] = jnp.zeros_like(o_ref)
        gate_up(a0_ref)

    @pl.when((f > 0) & (f < nf) & ((f & 1) == 1))
    def _():
        gate_up(a1_ref)
        down(a0_ref)

    @pl.when((f > 0) & (f < nf) & ((f & 1) == 0))
    def _():
        gate_up(a0_ref)
        down(a1_ref)

    @pl.when(f == nf)
    def _():
        down(a0_ref if (nf - 1) % 2 == 0 else a1_ref)
        o_ref[...] = x_ref[...] + FFN_RES * _rms(o_ref[...], gpost_ref[...])


def _ffn(x, gpre, w_up, wd, gpost):
    m = x.shape[0]
    cast = wd.dtype == F32
    tm = min(m, 512 if cast else 1024)
    nf = D_FF // FFN_TF
    row = lambda i, f: (i, 0)
    const = lambda i, f: (0, 0)
    out_specs = [pl.BlockSpec((tm, D_MODEL), row)]
    out_shape = [jax.ShapeDtypeStruct((m, D_MODEL), F32)]
    if cast:
        assert m == tm, "the casting variant writes each weight tile once"
        tf = FFN_TF
        body = _ffn_cast_body
        steps = nf
        up_spec = pl.BlockSpec((D_MODEL, tf), lambda i, f: (0, f))
        up_specs = [up_spec, up_spec]
        up_args = list(w_up)
        down_spec = pl.BlockSpec((tf, D_MODEL), lambda i, f: (f, 0))
        out_specs += [pl.BlockSpec((1, D_MODEL, 2 * tf), lambda i, f: (f, 0, 0)), down_spec]
        out_shape += [jax.ShapeDtypeStruct((nf, D_MODEL, 2 * tf), BF16), jax.ShapeDtypeStruct(wd.shape, BF16)]
        scratch = []
    else:
        tf = FFN_TF
        body = functools.partial(_ffn_skew_body, nf=nf)
        steps = nf + 1
        up_specs = [pl.BlockSpec((1, D_MODEL, 2 * tf), lambda i, f: (jnp.minimum(f, nf - 1), 0, 0))]
        up_args = [w_up]
        down_spec = pl.BlockSpec((tf, D_MODEL), lambda i, f: (jnp.maximum(f - 1, 0), 0))
        scratch = [pltpu.VMEM((tm, tf), BF16), pltpu.VMEM((tm, tf), BF16)]
    return pl.pallas_call(
        body,
        grid=(m // tm, steps),
        in_specs=[pl.BlockSpec((tm, D_MODEL), row), pl.BlockSpec((1, D_MODEL), const),
                  *up_specs, down_spec, pl.BlockSpec((1, D_MODEL), const)],
        out_specs=out_specs,
        out_shape=out_shape,
        scratch_shapes=[pltpu.VMEM((tm, D_MODEL), BF16)] + scratch,
        compiler_params=_params(("parallel", "arbitrary")),
        name="ffn_cast" if cast else "ffn",
    )(x, gpre, *up_args, wd, gpost)


def _proj_in_body(*refs, cast):
    if cast:
        x_ref, g_ref, w_ref, wgate_ref, proj_ref, gates_ref, w_o, h_ref = refs
    else:
        x_ref, g_ref, w_ref, wgate_ref, proj_ref, gates_ref, h_ref = refs
    nt = (((1,), (1,)), ((), ()))

    @pl.when(pl.program_id(1) == 0)
    def _():
        h = _rms(x_ref[...], g_ref[...]).astype(BF16)
        h_ref[...] = h
        gates_ref[...] = lax.dot_general(h, wgate_ref[...], nt, preferred_element_type=F32)

    if cast:
        w_o[...] = w_ref[0].astype(BF16)
        w_ref = w_o
    proj_ref[...] = lax.dot_general(h_ref[...], w_ref[...], nt, preferred_element_type=F32)


def _proj_in(x, g, w_main_t, w_gate_t):
    m = x.shape[0]
    cast = w_main_t.dtype == F32
    tm = min(m, 1024)
    tn = 1024 if cast else 1536
    if cast:
        assert m == tm, "the casting variant writes each weight tile once"
        w_spec = pl.BlockSpec((1, tn, D_MODEL), lambda i, n: (0, n, 0))
    else:
        w_spec = pl.BlockSpec((tn, D_MODEL), lambda i, n: (n, 0))
    out_specs = [pl.BlockSpec((tm, tn), lambda i, n: (i, n)), pl.BlockSpec((tm, LANES), lambda i, n: (i, 0))]
    out_shape = [jax.ShapeDtypeStruct((m, D_MAIN), F32), jax.ShapeDtypeStruct((m, LANES), F32)]
    if cast:
        out_specs.append(pl.BlockSpec((tn, D_MODEL), lambda i, n: (n, 0)))
        out_shape.append(jax.ShapeDtypeStruct((D_MAIN, D_MODEL), BF16))
    return pl.pallas_call(
        functools.partial(_proj_in_body, cast=cast),
        grid=(m // tm, D_MAIN // tn),
        in_specs=[
            pl.BlockSpec((tm, D_MODEL), lambda i, n: (i, 0)),
            pl.BlockSpec((1, D_MODEL), lambda i, n: (0, 0)),
            w_spec,
            pl.BlockSpec((LANES, D_MODEL), lambda i, n: (0, 0)),
        ],
        out_specs=out_specs,
        out_shape=out_shape,
        scratch_shapes=[pltpu.VMEM((tm, D_MODEL), BF16)],
        compiler_params=_params(("parallel", "arbitrary")),
        name="proj_in_cast" if cast else "proj_in",
    )(x, g, w_main_t, w_gate_t)


def _log_sigmoid(x):
    return jnp.minimum(x, 0.0) - jnp.log1p(jnp.exp(-jnp.abs(x)))


def _seg_cumsum(x, seg_len):
    pos = lax.broadcasted_iota(jnp.int32, x.shape, 0) & (seg_len - 1)
    shift = 1
    while shift < seg_len:
        x = x + jnp.where(pos >= shift, pltpu.roll(x, shift, 0), 0.0)
        shift *= 2
    return x


def _gate_terms(pre, seg_len):
    bt = _seg_cumsum(_log_sigmoid(pre), seg_len)
    return pre, bt, pre.T, bt.T


def _tile_masks(seg_len):
    log2 = seg_len.bit_length() - 1
    t_idx = lax.broadcasted_iota(jnp.int32, (ROWS, ROWS), 0)
    s_idx = lax.broadcasted_iota(jnp.int32, (ROWS, ROWS), 1)
    if seg_len == ROWS:
        return s_idx <= t_idx, None, None
    same = (t_idx >> log2) == (s_idx >> log2)
    last = s_idx == ((t_idx >> log2) << log2) + (seg_len - 1)
    return same & (s_idx <= t_idx), same, last


def _pick(x, j, axis):
    if isinstance(j, int):
        return x[:, j:j + 1] if axis == 1 else x[j:j + 1, :]
    idx = lax.broadcasted_iota(jnp.int32, x.shape, axis)
    return jnp.sum(jnp.where(idx == j, x, 0.0), axis=axis, keepdims=True)


def _mlstm_tile(q, k, v, gates, masks, hd, seg_len, m_prev):
    pre, bt_all, pre_t, bt_t = gates
    valid, same, last = masks
    ig_col = _pick(pre, hd, 1)
    bt_col = _pick(bt_all, hd + N_HEADS, 1)
    key_w = _pick(pre_t, hd, 0) - _pick(bt_t, hd + N_HEADS, 0)

    d = jnp.where(valid, bt_col + key_w, -jnp.inf)
    inter = bt_col + m_prev
    m_t = jnp.maximum(inter, jnp.max(d, axis=1, keepdims=True))
    w_intra = jnp.exp(d - m_t)
    w_inter = jnp.exp(inter - m_t)

    if seg_len == ROWS:
        bt_last = bt_col[ROWS - 1:ROWS, :]
        m_end = m_t[ROWS - 1:ROWS, :]
    else:
        bt_row = _pick(bt_t, hd + N_HEADS, 0)
        bt_last = jnp.sum(jnp.where(last, bt_row, 0.0), axis=1, keepdims=True)
        e = jnp.where(same, bt_last + key_w, -jnp.inf)
        m_end = jnp.maximum(bt_last + m_prev, jnp.max(e, axis=1, keepdims=True))
    g_keys = jnp.exp(bt_last - bt_col + ig_col - m_end)
    g_state = jnp.exp(bt_last + m_prev - m_end)

    ks = k * K_SCALE
    q_bf = q.astype(BF16)
    v_bf = v.astype(BF16)
    s = lax.dot_general(q_bf, ks.astype(BF16), (((1,), (1,)), ((), ())),
                        preferred_element_type=F32) * w_intra
    num = jnp.dot(s.astype(BF16), v_bf, preferred_element_type=F32)
    den = jnp.sum(s, axis=1, keepdims=True)
    kg = ks * g_keys
    return dict(q_bf=q_bf, v_bf=v_bf, num=num, den=den, kg=kg, m_t=m_t, w_inter=w_inter,
                g_state=g_state, m_end=m_end)


def _mlstm_out(t, q, o, q_c, q_n):
    num = t["num"] + q_c * t["w_inter"]
    den = t["den"] + q_n * t["w_inter"]
    h = num / jnp.maximum(jnp.abs(den), jnp.exp(-t["m_t"]))
    return (jax.nn.sigmoid(o) * h).astype(BF16)


def _mix_prompt_body(q_ref, k_ref, v_ref, o_ref, gates_ref, bias_ref, hc_ref, x_ref, wa_ref, wb_ref, g_ref,
                     y_ref, c_out_ref, n_out_ref, m_out_ref, c_ref, n_ref, m_ref, hm_ref, *, n_chunks, nt):
    t = pl.program_id(1)

    def project():
        mix = (jnp.dot(hm_ref[(t - 1) & 1], wa_ref[...], preferred_element_type=F32)
               + jnp.dot(hc_ref[...], wb_ref[...], preferred_element_type=F32))
        y_ref[...] = x_ref[...] + _rms(mix, g_ref[...])

    def recur():
        slot = t & 1
        masks = _tile_masks(ROWS)
        for c in range(n_chunks):
            rows = pl.ds(c * ROWS, ROWS)
            gates = _gate_terms(gates_ref[rows, :] + bias_ref[...], ROWS)
            for hd in range(N_HEADS):
                cols = pl.ds(hd * HEAD_DIM, HEAD_DIM)
                q = q_ref[rows, cols]
                tl = _mlstm_tile(q, k_ref[rows, cols], v_ref[rows, cols], gates, masks, hd, ROWS,
                                 m_ref[hd][:, 0:1])
                c_old = c_ref[hd]
                n_old = n_ref[hd]
                q_c = jnp.dot(tl["q_bf"], c_old.astype(BF16), preferred_element_type=F32)
                q_n = jnp.sum(q * n_old, axis=1, keepdims=True)
                hm_ref[slot, rows, cols] = _mlstm_out(tl, q, o_ref[rows, cols], q_c, q_n)
                g = tl["g_state"]
                c_ref[hd] = g * c_old + lax.dot_general(
                    tl["kg"].astype(BF16), tl["v_bf"], (((0,), (0,)), ((), ())), preferred_element_type=F32)
                n_ref[hd] = g * n_old + jnp.sum(tl["kg"], axis=0, keepdims=True)
                m_ref[hd] = jnp.broadcast_to(tl["m_end"], (1, LANES))

    @pl.when(t == 0)
    def _():
        c_ref[...] = jnp.zeros_like(c_ref)
        n_ref[...] = jnp.zeros_like(n_ref)
        m_ref[...] = jnp.zeros_like(m_ref)
        recur()

    @pl.when((t > 0) & (t < nt))
    def _():
        project()
        recur()

    @pl.when(t == nt)
    def _():
        project()
        c_out_ref[0] = c_ref[...]
        n_out_ref[0] = n_ref[...]
        m_out_ref[0] = m_ref[...]


def _mix_prompt(proj, gates, bias, hc, x, w_out, g, batch, seq, *, tt=256):
    nt = seq // tt
    cur = lambda j: (lambda b, t: (b * nt + jnp.minimum(t, nt - 1), j))
    prev = lambda b, t: (b * nt + jnp.maximum(t - 1, 0), 0)
    state = lambda b, t: (b, 0, 0, 0)
    const = lambda b, t: (0, 0)
    return pl.pallas_call(
        functools.partial(_mix_prompt_body, n_chunks=tt // ROWS, nt=nt),
        grid=(batch, nt + 1),
        in_specs=[
            pl.BlockSpec((tt, D_MLSTM), cur(0)),
            pl.BlockSpec((tt, D_MLSTM), cur(1)),
            pl.BlockSpec((tt, D_MLSTM), cur(2)),
            pl.BlockSpec((tt, D_MLSTM), cur(3)),
            pl.BlockSpec((tt, LANES), cur(0)),
            pl.BlockSpec((1, LANES), const),
            pl.BlockSpec((tt, D_CONV), prev),
            pl.BlockSpec((tt, D_MODEL), prev),
            pl.BlockSpec((D_MLSTM, D_MODEL), lambda b, t: (0, 0)),
            pl.BlockSpec((D_CONV, D_MODEL), lambda b, t: (1, 0)),
            pl.BlockSpec((1, D_MODEL), const),
        ],
        out_specs=[
            pl.BlockSpec((tt, D_MODEL), prev),
            pl.BlockSpec((1, N_HEADS, HEAD_DIM, HEAD_DIM), state),
            pl.BlockSpec((1, N_HEADS, 1, HEAD_DIM), state),
            pl.BlockSpec((1, N_HEADS, 1, LANES), state),
        ],
        out_shape=[
            jax.ShapeDtypeStruct((batch * seq, D_MODEL), F32),
            jax.ShapeDtypeStruct((batch, N_HEADS, HEAD_DIM, HEAD_DIM), F32),
            jax.ShapeDtypeStruct((batch, N_HEADS, 1, HEAD_DIM), F32),
            jax.ShapeDtypeStruct((batch, N_HEADS, 1, LANES), F32),
        ],
        scratch_shapes=[pltpu.VMEM((N_HEADS, HEAD_DIM, HEAD_DIM), F32), pltpu.VMEM((N_HEADS, 1, HEAD_DIM), F32),
                        pltpu.VMEM((N_HEADS, 1, LANES), F32), pltpu.VMEM((2, tt, D_MLSTM), BF16)],
        compiler_params=_params(("parallel", "arbitrary")),
        name="mix_prompt",
    )(proj, proj, proj, proj, gates, bias, hc, x, w_out, w_out, g)


def _mlstm_sample_body(q_ref, k_ref, v_ref, o_ref, gates_ref, bias_ref, mrow_ref, c_ref, n_ref,
                       hm_ref, c_out_ref, n_out_ref, m_out_ref, *, seg_len):
    hd = pl.program_id(1)
    n_seg = ROWS // seg_len
    grp = 16 // seg_len
    log2 = seg_len.bit_length() - 1
    q = q_ref[...]
    gates = _gate_terms(gates_ref[...] + bias_ref[...], seg_len)
    t = _mlstm_tile(q, k_ref[...], v_ref[...], gates, _tile_masks(seg_len), hd, seg_len, mrow_ref[0])

    seg_of_row = lax.broadcasted_iota(jnp.int32, (16, 1), 0) >> log2
    qc_parts, n_parts = [], []
    for j in range(ROWS // 16):
        qg = t["q_bf"][16 * j:16 * (j + 1)]
        qc, nr = None, None
        for i in range(grp):
            b = grp * j + i
            r = jnp.dot(qg, c_ref[0, b, 0].astype(BF16), preferred_element_type=F32)
            nb = jnp.broadcast_to(n_ref[b, 0], (16, HEAD_DIM))
            qc = r if i == 0 else jnp.where(seg_of_row == i, r, qc)
            nr = nb if i == 0 else jnp.where(seg_of_row == i, nb, nr)
        qc_parts.append(qc)
        n_parts.append(nr)
    q_c = jnp.concatenate(qc_parts, axis=0)
    q_n = jnp.sum(q * jnp.concatenate(n_parts, axis=0), axis=1, keepdims=True)
    hm_ref[...] = _mlstm_out(t, q, o_ref[...], q_c, q_n)

    kg = t["kg"]
    kg_t = kg.T
    seg_of_lane = lax.broadcasted_iota(jnp.int32, (1, ROWS), 1) >> log2
    seg_of_row8 = lax.broadcasted_iota(jnp.int32, (8, 1), 0) >> log2
    per8 = 8 // seg_len
    for b in range(n_seg):
        g = t["g_state"][seg_len * b:seg_len * b + 1, :]
        upd = jnp.dot(jnp.where(seg_of_lane == b, kg_t, 0.0).astype(BF16), t["v_bf"],
                      preferred_element_type=F32)
        c_out_ref[0, b, 0] = g * c_ref[0, b, 0] + upd
        kg8 = kg[8 * (b // per8):8 * (b // per8) + 8]
        n_out_ref[b, 0] = g * n_ref[b, 0] + jnp.sum(
            jnp.where(seg_of_row8 == (b % per8), kg8, 0.0), axis=0, keepdims=True)
    m_out_ref[0] = t["m_end"]


def _mlstm_sample(proj, gates, bias, m_rows, c0, n0, seg_len):
    m = proj.shape[0]
    n_seg = ROWS // seg_len
    col = lambda j: (lambda i, h: (i, j * N_HEADS + h))
    return pl.pallas_call(
        functools.partial(_mlstm_sample_body, seg_len=seg_len),
        grid=(m // ROWS, N_HEADS),
        in_specs=[
            pl.BlockSpec((ROWS, HEAD_DIM), col(0)),
            pl.BlockSpec((ROWS, HEAD_DIM), col(1)),
            pl.BlockSpec((ROWS, HEAD_DIM), col(2)),
            pl.BlockSpec((ROWS, HEAD_DIM), col(3)),
            pl.BlockSpec((ROWS, LANES), lambda i, h: (i, 0)),
            pl.BlockSpec((1, LANES), lambda i, h: (0, 0)),
            pl.BlockSpec((1, ROWS, 1), lambda i, h: (h, i, 0)),
            pl.BlockSpec((1, n_seg, 1, HEAD_DIM, HEAD_DIM), lambda i, h: (0, i, h, 0, 0)),
            pl.BlockSpec((n_seg, 1, 1, HEAD_DIM), lambda i, h: (i, h, 0, 0)),
        ],
        out_specs=[
            pl.BlockSpec((ROWS, HEAD_DIM), lambda i, h: (i, h)),
            pl.BlockSpec((1, n_seg, 1, HEAD_DIM, HEAD_DIM), lambda i, h: (0, i, h, 0, 0)),
            pl.BlockSpec((n_seg, 1, 1, HEAD_DIM), lambda i, h: (i, h, 0, 0)),
            pl.BlockSpec((1, ROWS, 1), lambda i, h: (h, i, 0)),
        ],
        out_shape=[
            jax.ShapeDtypeStruct((m, D_MLSTM), BF16),
            jax.ShapeDtypeStruct(c0.shape, F32),
            jax.ShapeDtypeStruct(n0.shape, F32),
            jax.ShapeDtypeStruct(m_rows.shape, F32),
        ],
        compiler_params=_params(("parallel", "parallel")),
        name="mlstm_sample",
    )(proj, proj, proj, proj, gates, bias, m_rows, c0, n0)


def _ln_swish(y, g, b):
    mu = jnp.mean(y, axis=-1, keepdims=True)
    yc = y - mu
    var = jnp.mean(yc * yc, axis=-1, keepdims=True)
    z = yc * lax.rsqrt(var + EPS) * g + b
    return z * jax.nn.sigmoid(z)


def _conv_prompt_body(a_ref, b_ref, w_ref, bdw_ref, gln_ref, bln_ref, hc_ref, st_ref,
                      u_ref, us_ref, wb_ref, y_ref, *, tt):
    t_id = pl.program_id(1)
    n_shift = tt + CONV_PAD - 8

    @pl.when(t_id == 0)
    def _():
        u_ref[0:CONV_PAD, :] = jnp.zeros((CONV_PAD, D_CONV), F32)
        for s in range(CONV_WIDTH):
            wb_ref[s] = jnp.broadcast_to(w_ref[s:s + 1, :], (8, D_CONV))

    u_ref[CONV_PAD:CONV_PAD + tt, :] = a_ref[...] * jax.nn.sigmoid(b_ref[...])
    for r in range(1, 8):
        us_ref[r - 1] = u_ref[r:r + n_shift, :]

    def row_block(i, carry):
        base = pl.multiple_of(i * CONV_ROWS, CONV_ROWS)
        n_slab = CONV_ROWS // 8
        acc = [jnp.broadcast_to(bdw_ref[...], (8, D_CONV))] * n_slab
        for s in range(CONV_WIDTH):
            k8, r = divmod(CONV_PAD - HALO + s, 8)
            w = wb_ref[s]
            for j in range(n_slab):
                rows = pl.ds(pl.multiple_of(base + 8 * (k8 + j), 8), 8)
                win = u_ref[rows, :] if r == 0 else us_ref[r - 1, rows, :]
                acc[j] = acc[j] + w * win
        for j in range(n_slab):
            y_ref[pl.ds(pl.multiple_of(base + 8 * j, 8), 8), :] = acc[j]
        return carry

    lax.fori_loop(0, tt // CONV_ROWS, row_block, 0)
    hc_ref[...] = _ln_swish(y_ref[...], gln_ref[...], bln_ref[...]).astype(BF16)

    @pl.when(t_id == pl.num_programs(1) - 1)
    def _():
        st_ref[0, 0] = u_ref[CONV_PAD + tt - HALO:CONV_PAD + tt, :]

    u_ref[0:CONV_PAD, :] = u_ref[tt:tt + CONV_PAD, :]


def _conv_prompt(proj, w_dw, b_dw, g_ln, b_ln, batch, seq, *, tt=256):
    nt = seq // tt
    const = lambda b, t: (0, 0)
    return pl.pallas_call(
        functools.partial(_conv_prompt_body, tt=tt),
        grid=(batch, nt),
        in_specs=[
            pl.BlockSpec((tt, D_CONV), lambda b, t: (b * nt + t, 4)),
            pl.BlockSpec((tt, D_CONV), lambda b, t: (b * nt + t, 5)),
            pl.BlockSpec((CONV_WIDTH, D_CONV), const),
            pl.BlockSpec((1, D_CONV), const),
            pl.BlockSpec((1, D_CONV), const),
            pl.BlockSpec((1, D_CONV), const),
        ],
        out_specs=[
            pl.BlockSpec((tt, D_CONV), lambda b, t: (b * nt + t, 0)),
            pl.BlockSpec((1, 1, HALO, D_CONV), lambda b, t: (0, b, 0, 0)),
        ],
        out_shape=[
            jax.ShapeDtypeStruct((batch * seq, D_CONV), BF16),
            jax.ShapeDtypeStruct((1, batch, HALO, D_CONV), F32),
        ],
        scratch_shapes=[pltpu.VMEM((tt + CONV_PAD, D_CONV), F32),
                        pltpu.VMEM((7, tt + CONV_PAD - 8, D_CONV), F32),
                        pltpu.VMEM((CONV_WIDTH, 8, D_CONV), F32),
                        pltpu.VMEM((tt, D_CONV), F32)],
        compiler_params=_params(("parallel", "arbitrary")),
        name="conv_prompt",
    )(proj, proj, w_dw, b_dw, g_ln, b_ln)


def _conv_sample_body(a_ref, b_ref, st_ref, w_ref, bdw_ref, gln_ref, bln_ref, hc_ref, st_out_ref,
                      wb_ref, *, seq):
    @pl.when(pl.program_id(0) == 0)
    def _():
        for s in range(CONV_WIDTH):
            wb_ref[s] = jnp.broadcast_to(w_ref[s:s + 1, :], (8, D_CONV))

    acc = [jnp.broadcast_to(bdw_ref[...], (8, D_CONV))] * seq
    for j in range(HALO + seq):
        slab = st_ref[0, j] if j < HALO else a_ref[j - HALO] * jax.nn.sigmoid(b_ref[j - HALO])
        for t in range(seq):
            if 0 <= j - t < CONV_WIDTH:
                acc[t] = acc[t] + wb_ref[j - t] * slab
        if j >= seq:
            st_out_ref[0, j - seq] = slab
    for t in range(seq):
        hc_ref[t] = _ln_swish(acc[t], gln_ref[...], bln_ref[...])


def _conv_sample(a_t, b_t, state_t, w_dw, b_dw, g_ln, b_ln):
    seq, batch, _ = a_t.shape
    bb = 8
    const = lambda i: (0, 0)
    tok = pl.BlockSpec((seq, bb, D_CONV), lambda i: (0, i, 0))
    hist = pl.BlockSpec((1, HALO, bb, D_CONV), lambda i: (0, 0, i, 0))
    return pl.pallas_call(
        functools.partial(_conv_sample_body, seq=seq),
        grid=(batch // bb,),
        in_specs=[tok, tok, hist,
                  pl.BlockSpec((CONV_WIDTH, D_CONV), const),
                  pl.BlockSpec((1, D_CONV), const),
                  pl.BlockSpec((1, D_CONV), const),
                  pl.BlockSpec((1, D_CONV), const)],
        out_specs=[tok, hist],
        out_shape=[
            jax.ShapeDtypeStruct(a_t.shape, F32),
            jax.ShapeDtypeStruct(state_t.shape, F32),
        ],
        scratch_shapes=[pltpu.VMEM((CONV_WIDTH, 8, D_CONV), F32)],
        compiler_params=_params(("arbitrary",)),
        name="conv_sample",
    )(a_t, b_t, state_t, w_dw, b_dw, g_ln, b_ln)


def _proj_out_body(hm_ref, hc_ref, x_ref, wa_ref, wb_ref, g_ref, o_ref):
    mix = (jnp.dot(hm_ref[...], wa_ref[...], preferred_element_type=F32)
           + jnp.dot(hc_ref[...].astype(BF16), wb_ref[...], preferred_element_type=F32))
    o_ref[...] = x_ref[...] + _rms(mix, g_ref[...])


def _proj_out(hm, hc, x, w_out, g, *, tm=512):
    m = x.shape[0]
    row = lambda i: (i, 0)
    return pl.pallas_call(
        _proj_out_body,
        grid=(m // tm,),
        in_specs=[
            pl.BlockSpec((tm, D_MLSTM), row),
            pl.BlockSpec((tm, D_CONV), row),
            pl.BlockSpec((tm, D_MODEL), row),
            pl.BlockSpec((D_MLSTM, D_MODEL), lambda i: (0, 0)),
            pl.BlockSpec((D_CONV, D_MODEL), lambda i: (1, 0)),
            pl.BlockSpec((1, D_MODEL), lambda i: (0, 0)),
        ],
        out_specs=pl.BlockSpec((tm, D_MODEL), row),
        out_shape=jax.ShapeDtypeStruct((m, D_MODEL), F32),
        compiler_params=_params(("parallel",)),
        name="proj_out",
    )(hm, hc, x, w_out, w_out, g)


def kernel(x_prompt, x_sample, state_mlstm_C, state_mlstm_n, state_mlstm_m, state_conv, g_ffn1_pre, w_ffn1_gate, w_ffn1_up, w_ffn1_down, g_ffn1_post, g_mix_pre, w_in, b_igate, b_fgate, w_dw, b_dw, g_conv_ln, b_conv_ln, w_out, g_mix_post, g_ffn2_pre, w_ffn2_gate, w_ffn2_up, w_ffn2_down, g_ffn2_post):
    depth = state_mlstm_C.shape[0]
    assert depth == 1, "kernel handles a single layer"
    bp, tp, _ = x_prompt.shape
    bs, ts, _ = x_sample.shape
    l = 0

    w_in_t = jnp.swapaxes(w_in, 1, 2)
    w_gate_t = jnp.pad(w_in_t[l, D_MAIN:], ((0, LANES - 2 * N_HEADS), (0, 0))).astype(BF16)
    w_o = w_out[l].astype(BF16)
    bias = jnp.pad(jnp.concatenate([b_igate[l], b_fgate[l]]), (0, LANES - 2 * N_HEADS))[None, :]

    xs, *ffn1 = _ffn(x_sample.reshape(bs * ts, D_MODEL), g_ffn1_pre,
                     (w_ffn1_gate[l], w_ffn1_up[l]), w_ffn1_down[l], g_ffn1_post)
    proj_s, gates_s, w_main_t = _proj_in(xs, g_mix_pre, w_in_t, w_gate_t)
    m_rows = jnp.repeat(state_mlstm_m[l].T, ts, axis=1)[:, :, None]
    hm_s, c_s, n_s, m_s = _mlstm_sample(
        proj_s, gates_s, bias, m_rows, state_mlstm_C, state_mlstm_n[l][:, :, None, :], ts)
    glu_t = jnp.swapaxes(proj_s[:, 4 * D_MLSTM:].reshape(bs, ts, 2, D_CONV), 0, 1)
    hc_t, conv_t = _conv_sample(glu_t[:, :, 0], glu_t[:, :, 1], jnp.swapaxes(state_conv, 1, 2),
                                w_dw[l], b_dw, g_conv_ln, b_conv_ln)
    hc_s = jnp.swapaxes(hc_t, 0, 1).reshape(bs * ts, D_CONV)
    xs = _proj_out(hm_s, hc_s, xs, w_o, g_mix_post)
    ys, *ffn2 = _ffn(xs, g_ffn2_pre, (w_ffn2_gate[l], w_ffn2_up[l]), w_ffn2_down[l], g_ffn2_post)
    ys = ys.reshape(bs, ts, D_MODEL)

    xp = _ffn(x_prompt.reshape(bp * tp, D_MODEL), g_ffn1_pre, *ffn1, g_ffn1_post)[0]
    proj_p, gates_p = _proj_in(xp, g_mix_pre, w_main_t, w_gate_t)
    hc_p, conv_p = _conv_prompt(proj_p, w_dw[l], b_dw, g_conv_ln, b_conv_ln, bp, tp)
    xp, c_p, n_p, m_p = _mix_prompt(proj_p, gates_p, bias, hc_p, xp, w_o, g_mix_post, bp, tp)
    yp = _ffn(xp, g_ffn2_pre, *ffn2, g_ffn2_post)[0].reshape(bp, tp, D_MODEL)

    return (yp, ys,
            c_p[None], n_p[:, :, 0, :][None], m_p[:, :, 0, 0][None], conv_p,
            c_s, n_s[:, :, 0, :][None], m_s[:, ::ts, 0].T[None], jnp.swapaxes(conv_t, 1, 2))
```

```python
import functools

import jax
import jax.numpy as jnp
from jax import lax
from jax.experimental import pallas as pl
from jax.experimental.pallas import tpu as pltpu

D_MODEL = 2048
N_HEADS = 4
HEAD_DIM = 256
D_MLSTM = N_HEADS * HEAD_DIM
D_CONV = D_MODEL - D_MLSTM
CONV_WIDTH = 31
HALO = CONV_WIDTH - 1
CONV_PAD = 32
CONV_ROWS = 32
D_FF = 5632
FFN_TF = 512
D_MAIN = 4 * D_MLSTM + 2 * D_CONV
EPS = 1e-6
FFN_RES = 0.5
K_SCALE = HEAD_DIM ** -0.5

LANES = 128
ROWS = 128
VMEM_LIMIT = 56 * 1024 * 1024

F32 = jnp.float32
BF16 = jnp.bfloat16


def _params(sem):
    return pltpu.CompilerParams(dimension_semantics=sem, vmem_limit_bytes=VMEM_LIMIT)


def _rms(x, g):
    return x * lax.rsqrt(jnp.mean(x * x, axis=-1, keepdims=True) + EPS) * g


def _ffn_cast_body(x_ref, gpre_ref, wg_ref, wu_ref, wd_ref, gpost_ref, o_ref, wgu_o, wd_o,
                   h_ref):
    f = pl.program_id(1)

    @pl.when(f == 0)
    def _():
        h_ref[...] = _rms(x_ref[...], gpre_ref[...]).astype(BF16)
        o_ref[...] = jnp.zeros_like(o_ref)

    tf = wg_ref.shape[1]
    wgu_o[0, :, :tf] = wg_ref[...].astype(BF16)
    wgu_o[0, :, tf:] = wu_ref[...].astype(BF16)
    wd_o[...] = wd_ref[...].astype(BF16)
    o_ref[...] += jnp.dot(_swiglu_act(h_ref[...], wgu_o[0]), wd_o[...], preferred_element_type=F32)

    @pl.when(f == pl.num_programs(1) - 1)
    def _():
        o_ref[...] = x_ref[...] + FFN_RES * _rms(o_ref[...], gpost_ref[...])


def _swiglu_act(h, w_gate_up):
    gu = jnp.dot(h, w_gate_up, preferred_element_type=F32)
    tf = gu.shape[1] // 2
    g, u = gu[:, :tf], gu[:, tf:]
    return ((g * jax.nn.sigmoid(g)) * u).astype(BF16)


def _ffn_skew_body(x_ref, gpre_ref, wgu_ref, wd_ref, gpost_ref, o_ref,
                   h_ref, a0_ref, a1_ref, *, nf):
    f = pl.program_id(1)

    def gate_up(a_ref):
        a_ref[...] = _swiglu_act(h_ref[...], wgu_ref[0])

    def down(a_ref):
        o_ref[...] += jnp.dot(a_ref[...], wd_ref[...], preferred_element_type=F32)

    @pl.when(f == 0)
    def _():
        h_ref[...] = _rms(x_ref[...], gpre_ref[...]).astype(BF16)
        o_ref[...] = jnp.zeros_like(o_ref)
        gate_up(a0_ref)

    @pl.when((f > 0) & (f < nf) & ((f & 1) == 1))
    def _():
        gate_up(a1_ref)
        down(a0_ref)

    @pl.when((f > 0) & (f < nf) & ((f & 1) == 0))
    def _():
        gate_up(a0_ref)
        down(a1_ref)

    @pl.when(f == nf)
    def _():
        down(a0_ref if (nf - 1) % 2 == 0 else a1_ref)
        o_ref[...] = x_ref[...] + FFN_RES * _rms(o_ref[...], gpost_ref[...])


def _ffn(x, gpre, w_up, wd, gpost):
    m = x.shape[0]
    cast = wd.dtype == F32
    tm = min(m, 512 if cast else 1024)
    nf = D_FF // FFN_TF
    row = lambda i, f: (i, 0)
    const = lambda i, f: (0, 0)
    out_specs = [pl.BlockSpec((tm, D_MODEL), row)]
    out_shape = [jax.ShapeDtypeStruct((m, D_MODEL), F32)]
    if cast:
        assert m == tm, "the casting variant writes each weight tile once"
        tf = FFN_TF
        body = _ffn_cast_body
        steps = nf
        up_spec = pl.BlockSpec((D_MODEL, tf), lambda i, f: (0, f))
        up_specs = [up_spec, up_spec]
        up_args = list(w_up)
        down_spec = pl.BlockSpec((tf, D_MODEL), lambda i, f: (f, 0))
        out_specs += [pl.BlockSpec((1, D_MODEL, 2 * tf), lambda i, f: (f, 0, 0)), down_spec]
        out_shape += [jax.ShapeDtypeStruct((nf, D_MODEL, 2 * tf), BF16), jax.ShapeDtypeStruct(wd.shape, BF16)]
        scratch = []
    else:
        tf = FFN_TF
        body = functools.partial(_ffn_skew_body, nf=nf)
        steps = nf + 1
        up_specs = [pl.BlockSpec((1, D_MODEL, 2 * tf), lambda i, f: (jnp.minimum(f, nf - 1), 0, 0))]
        up_args = [w_up]
        down_spec = pl.BlockSpec((tf, D_MODEL), lambda i, f: (jnp.maximum(f - 1, 0), 0))
        scratch = [pltpu.VMEM((tm, tf), BF16), pltpu.VMEM((tm, tf), BF16)]
    return pl.pallas_call(
        body,
        grid=(m // tm, steps),
        in_specs=[pl.BlockSpec((tm, D_MODEL), row), pl.BlockSpec((1, D_MODEL), const),
                  *up_specs, down_spec, pl.BlockSpec((1, D_MODEL), const)],
        out_specs=out_specs,
        out_shape=out_shape,
        scratch_shapes=[pltpu.VMEM((tm, D_MODEL), BF16)] + scratch,
        compiler_params=_params(("parallel", "arbitrary")),
        name="ffn_cast" if cast else "ffn",
    )(x, gpre, *up_args, wd, gpost)


def _proj_in_body(*refs, cast):
    if cast:
        x_ref, g_ref, w_ref, wgate_ref, proj_ref, gates_ref, w_o, h_ref = refs
    else:
        x_ref, g_ref, w_ref, wgate_ref, proj_ref, gates_ref, h_ref = refs
    nt = (((1,), (1,)), ((), ()))

    @pl.when(pl.program_id(1) == 0)
    def _():
        h = _rms(x_ref[...], g_ref[...]).astype(BF16)
        h_ref[...] = h
        gates_ref[...] = lax.dot_general(h, wgate_ref[...], nt, preferred_element_type=F32)

    if cast:
        w_o[...] = w_ref[0].astype(BF16)
        w_ref = w_o
    proj_ref[...] = lax.dot_general(h_ref[...], w_ref[...], nt, preferred_element_type=F32)


def _proj_in(x, g, w_main_t, w_gate_t):
    m = x.shape[0]
    cast = w_main_t.dtype == F32
    tm = min(m, 1024)
    tn = 1024 if cast else 1536
    if cast:
        assert m == tm, "the casting variant writes each weight tile once"
        w_spec = pl.BlockSpec((1, tn, D_MODEL), lambda i, n: (0, n, 0))
    else:
        w_spec = pl.BlockSpec((tn, D_MODEL), lambda i, n: (n, 0))
    out_specs = [pl.BlockSpec((tm, tn), lambda i, n: (i, n)), pl.BlockSpec((tm, LANES), lambda i, n: (i, 0))]
    out_shape = [jax.ShapeDtypeStruct((m, D_MAIN), F32), jax.ShapeDtypeStruct((m, LANES), F32)]
    if cast:
        out_specs.append(pl.BlockSpec((tn, D_MODEL), lambda i, n: (n, 0)))
        out_shape.append(jax.ShapeDtypeStruct((D_MAIN, D_MODEL), BF16))
    return pl.pallas_call(
        functools.partial(_proj_in_body, cast=cast),
        grid=(m // tm, D_MAIN // tn),
        in_specs=[
            pl.BlockSpec((tm, D_MODEL), lambda i, n: (i, 0)),
            pl.BlockSpec((1, D_MODEL), lambda i, n: (0, 0)),
            w_spec,
            pl.BlockSpec((LANES, D_MODEL), lambda i, n: (0, 0)),
        ],
        out_specs=out_specs,
        out_shape=out_shape,
        scratch_shapes=[pltpu.VMEM((tm, D_MODEL), BF16)],
        compiler_params=_params(("parallel", "arbitrary")),
        name="proj_in_cast" if cast else "proj_in",
    )(x, g, w_main_t, w_gate_t)


def _log_sigmoid(x):
    return jnp.minimum(x, 0.0) - jnp.log1p(jnp.exp(-jnp.abs(x)))


def _seg_cumsum(x, seg_len):
    pos = lax.broadcasted_iota(jnp.int32, x.shape, 0) & (seg_len - 1)
    shift = 1
    while shift < seg_len:
        x = x + jnp.where(pos >= shift, pltpu.roll(x, shift, 0), 0.0)
        shift *= 2
    return x


def _gate_terms(pre, seg_len):
    bt = _seg_cumsum(_log_sigmoid(pre), seg_len)
    return pre, bt, pre.T, bt.T


def _tile_masks(seg_len):
    log2 = seg_len.bit_length() - 1
    t_idx = lax.broadcasted_iota(jnp.int32, (ROWS, ROWS), 0)
    s_idx = lax.broadcasted_iota(jnp.int32, (ROWS, ROWS), 1)
    if seg_len == ROWS:
        return s_idx <= t_idx, None, None
    same = (t_idx >> log2) == (s_idx >> log2)
    last = s_idx == ((t_idx >> log2) << log2) + (seg_len - 1)
    return same & (s_idx <= t_idx), same, last


def _pick(x, j, axis):
    if isinstance(j, int):
        return x[:, j:j + 1] if axis == 1 else x[j:j + 1, :]
    idx = lax.broadcasted_iota(jnp.int32, x.shape, axis)
    return jnp.sum(jnp.where(idx == j, x, 0.0), axis=axis, keepdims=True)


def _mlstm_tile(q, k, v, gates, masks, hd, seg_len, m_prev):
    pre, bt_all, pre_t, bt_t = gates
    valid, same, last = masks
    ig_col = _pick(pre, hd, 1)
    bt_col = _pick(bt_all, hd + N_HEADS, 1)
    key_w = _pick(pre_t, hd, 0) - _pick(bt_t, hd + N_HEADS, 0)

    d = jnp.where(valid, bt_col + key_w, -jnp.inf)
    inter = bt_col + m_prev
    m_t = jnp.maximum(inter, jnp.max(d, axis=1, keepdims=True))
    w_intra = jnp.exp(d - m_t)
    w_inter = jnp.exp(inter - m_t)

    if seg_len == ROWS:
        bt_last = bt_col[ROWS - 1:ROWS, :]
        m_end = m_t[ROWS - 1:ROWS, :]
    else:
        bt_row = _pick(bt_t, hd + N_HEADS, 0)
        bt_last = jnp.sum(jnp.where(last, bt_row, 0.0), axis=1, keepdims=True)
        e = jnp.where(same, bt_last + key_w, -jnp.inf)
        m_end = jnp.maximum(bt_last + m_prev, jnp.max(e, axis=1, keepdims=True))
    g_keys = jnp.exp(bt_last - bt_col + ig_col - m_end)
    g_state = jnp.exp(bt_last + m_prev - m_end)

    ks = k * K_SCALE
    q_bf = q.astype(BF16)
    v_bf = v.astype(BF16)
    s = lax.dot_general(q_bf, ks.astype(BF16), (((1,), (1,)), ((), ())),
                        preferred_element_type=F32) * w_intra
    num = jnp.dot(s.astype(BF16), v_bf, preferred_element_type=F32)
    den = jnp.sum(s, axis=1, keepdims=True)
    kg = ks * g_keys
    return dict(q_bf=q_bf, v_bf=v_bf, num=num, den=den, kg=kg, m_t=m_t, w_inter=w_inter,
                g_state=g_state, m_end=m_end)


def _mlstm_out(t, q, o, q_c, q_n):
    num = t["num"] + q_c * t["w_inter"]
    den = t["den"] + q_n * t["w_inter"]
    h = num / jnp.maximum(jnp.abs(den), jnp.exp(-t["m_t"]))
    return (jax.nn.sigmoid(o) * h).astype(BF16)


def _mix_prompt_body(q_ref, k_ref, v_ref, o_ref, gates_ref, bias_ref, hc_ref, x_ref, wa_ref, wb_ref, g_ref,
                     y_ref, c_out_ref, n_out_ref, m_out_ref, c_ref, n_ref, m_ref, hm_ref, *, n_chunks, nt):
    t = pl.program_id(1)

    def project():
        mix = (jnp.dot(hm_ref[(t - 1) & 1], wa_ref[...], preferred_element_type=F32)
               + jnp.dot(hc_ref[...], wb_ref[...], preferred_element_type=F32))
        y_ref[...] = x_ref[...] + _rms(mix, g_ref[...])

    def recur():
        slot = t & 1
        masks = _tile_masks(ROWS)
        for c in range(n_chunks):
            rows = pl.ds(c * ROWS, ROWS)
            gates = _gate_terms(gates_ref[rows, :] + bias_ref[...], ROWS)
            for hd in range(N_HEADS):
                cols = pl.ds(hd * HEAD_DIM, HEAD_DIM)
                q = q_ref[rows, cols]
                tl = _mlstm_tile(q, k_ref[rows, cols], v_ref[rows, cols], gates, masks, hd, ROWS,
                                 m_ref[hd][:, 0:1])
                c_old = c_ref[hd]
                n_old = n_ref[hd]
                q_c = jnp.dot(tl["q_bf"], c_old.astype(BF16), preferred_element_type=F32)
                q_n = jnp.sum(q * n_old, axis=1, keepdims=True)
                hm_ref[slot, rows, cols] = _mlstm_out(tl, q, o_ref[rows, cols], q_c, q_n)
                g = tl["g_state"]
                c_ref[hd] = g * c_old + lax.dot_general(
                    tl["kg"].astype(BF16), tl["v_bf"], (((0,), (0,)), ((), ())), preferred_element_type=F32)
                n_ref[hd] = g * n_old + jnp.sum(tl["kg"], axis=0, keepdims=True)
                m_ref[hd] = jnp.broadcast_to(tl["m_end"], (1, LANES))

    @pl.when(t == 0)
    def _():
        c_ref[...] = jnp.zeros_like(c_ref)
        n_ref[...] = jnp.zeros_like(n_ref)
        m_ref[...] = jnp.zeros_like(m_ref)
        recur()

    @pl.when((t > 0) & (t < nt))
    def _():
        project()
        recur()

    @pl.when(t == nt)
    def _():
        project()
        c_out_ref[0] = c_ref[...]
        n_out_ref[0] = n_ref[...]
        m_out_ref[0] = m_ref[...]


def _mix_prompt(proj, gates, bias, hc, x, w_out, g, batch, seq, *, tt=256):
    nt = seq // tt
    cur = lambda j: (lambda b, t: (b * nt + jnp.minimum(t, nt - 1), j))
    prev = lambda b, t: (b * nt + jnp.maximum(t - 1, 0), 0)
    state = lambda b, t: (b, 0, 0, 0)
    const = lambda b, t: (0, 0)
    return pl.pallas_call(
        functools.partial(_mix_prompt_body, n_chunks=tt // ROWS, nt=nt),
        grid=(batch, nt + 1),
        in_specs=[
            pl.BlockSpec((tt, D_MLSTM), cur(0)),
            pl.BlockSpec((tt, D_MLSTM), cur(1)),
            pl.BlockSpec((tt, D_MLSTM), cur(2)),
            pl.BlockSpec((tt, D_MLSTM), cur(3)),
            pl.BlockSpec((tt, LANES), cur(0)),
            pl.BlockSpec((1, LANES), const),
            pl.BlockSpec((tt, D_CONV), prev),
            pl.BlockSpec((tt, D_MODEL), prev),
            pl.BlockSpec((D_MLSTM, D_MODEL), lambda b, t: (0, 0)),
            pl.BlockSpec((D_CONV, D_MODEL), lambda b, t: (1, 0)),
            pl.BlockSpec((1, D_MODEL), const),
        ],
        out_specs=[
            pl.BlockSpec((tt, D_MODEL), prev),
            pl.BlockSpec((1, N_HEADS, HEAD_DIM, HEAD_DIM), state),
            pl.BlockSpec((1, N_HEADS, 1, HEAD_DIM), state),
            pl.BlockSpec((1, N_HEADS, 1, LANES), state),
        ],
        out_shape=[
            jax.ShapeDtypeStruct((batch * seq, D_MODEL), F32),
            jax.ShapeDtypeStruct((batch, N_HEADS, HEAD_DIM, HEAD_DIM), F32),
            jax.ShapeDtypeStruct((batch, N_HEADS, 1, HEAD_DIM), F32),
            jax.ShapeDtypeStruct((batch, N_HEADS, 1, LANES), F32),
        ],
        scratch_shapes=[pltpu.VMEM((N_HEADS, HEAD_DIM, HEAD_DIM), F32), pltpu.VMEM((N_HEADS, 1, HEAD_DIM), F32),
                        pltpu.VMEM((N_HEADS, 1, LANES), F32), pltpu.VMEM((2, tt, D_MLSTM), BF16)],
        compiler_params=_params(("parallel", "arbitrary")),
        name="mix_prompt",
    )(proj, proj, proj, proj, gates, bias, hc, x, w_out, w_out, g)


def _mlstm_sample_body(q_ref, k_ref, v_ref, o_ref, gates_ref, bias_ref, mrow_ref, c_ref, n_ref,
                       hm_ref, c_out_ref, n_out_ref, m_out_ref, *, seg_len):
    hd = pl.program_id(1)
    n_seg = ROWS // seg_len
    grp = 16 // seg_len
    log2 = seg_len.bit_length() - 1
    q = q_ref[...]
    gates = _gate_terms(gates_ref[...] + bias_ref[...], seg_len)
    t = _mlstm_tile(q, k_ref[...], v_ref[...], gates, _tile_masks(seg_len), hd, seg_len, mrow_ref[0])

    seg_of_row = lax.broadcasted_iota(jnp.int32, (16, 1), 0) >> log2
    qc_parts, n_parts = [], []
    for j in range(ROWS // 16):
        qg = t["q_bf"][16 * j:16 * (j + 1)]
        qc, nr = None, None
        for i in range(grp):
            b = grp * j + i
            r = jnp.dot(qg, c_ref[0, b, 0].astype(BF16), preferred_element_type=F32)
            nb = jnp.broadcast_to(n_ref[b, 0], (16, HEAD_DIM))
            qc = r if i == 0 else jnp.where(seg_of_row == i, r, qc)
            nr = nb if i == 0 else jnp.where(seg_of_row == i, nb, nr)
        qc_parts.append(qc)
        n_parts.append(nr)
    q_c = jnp.concatenate(qc_parts, axis=0)
    q_n = jnp.sum(q * jnp.concatenate(n_parts, axis=0), axis=1, keepdims=True)
    hm_ref[...] = _mlstm_out(t, q, o_ref[...], q_c, q_n)

    kg = t["kg"]
    kg_t = kg.T
    seg_of_lane = lax.broadcasted_iota(jnp.int32, (1, ROWS), 1) >> log2
    seg_of_row8 = lax.broadcasted_iota(jnp.int32, (8, 1), 0) >> log2
    per8 = 8 // seg_len
    for b in range(n_seg):
        g = t["g_state"][seg_len * b:seg_len * b + 1, :]
        upd = jnp.dot(jnp.where(seg_of_lane == b, kg_t, 0.0).astype(BF16), t["v_bf"],
                      preferred_element_type=F32)
        c_out_ref[0, b, 0] = g * c_ref[0, b, 0] + upd
        kg8 = kg[8 * (b // per8):8 * (b // per8) + 8]
        n_out_ref[b, 0] = g * n_ref[b, 0] + jnp.sum(
            jnp.where(seg_of_row8 == (b % per8), kg8, 0.0), axis=0, keepdims=True)
    m_out_ref[0] = t["m_end"]


def _mlstm_sample(proj, gates, bias, m_rows, c0, n0, seg_len):
    m = proj.shape[0]
    n_seg = ROWS // seg_len
    col = lambda j: (lambda i, h: (i, j * N_HEADS + h))
    return pl.pallas_call(
        functools.partial(_mlstm_sample_body, seg_len=seg_len),
        grid=(m // ROWS, N_HEADS),
        in_specs=[
            pl.BlockSpec((ROWS, HEAD_DIM), col(0)),
            pl.BlockSpec((ROWS, HEAD_DIM), col(1)),
            pl.BlockSpec((ROWS, HEAD_DIM), col(2)),
            pl.BlockSpec((ROWS, HEAD_DIM), col(3)),
            pl.BlockSpec((ROWS, LANES), lambda i, h: (i, 0)),
            pl.BlockSpec((1, LANES), lambda i, h: (0, 0)),
            pl.BlockSpec((1, ROWS, 1), lambda i, h: (h, i, 0)),
            pl.BlockSpec((1, n_seg, 1, HEAD_DIM, HEAD_DIM), lambda i, h: (0, i, h, 0, 0)),
            pl.BlockSpec((n_seg, 1, 1, HEAD_DIM), lambda i, h: (i, h, 0, 0)),
        ],
        out_specs=[
            pl.BlockSpec((ROWS, HEAD_DIM), lambda i, h: (i, h)),
            pl.BlockSpec((1, n_seg, 1, HEAD_DIM, HEAD_DIM), lambda i, h: (0, i, h, 0, 0)),
            pl.BlockSpec((n_seg, 1, 1, HEAD_DIM), lambda i, h: (i, h, 0, 0)),
            pl.BlockSpec((1, ROWS, 1), lambda i, h: (h, i, 0)),
        ],
        out_shape=[
            jax.ShapeDtypeStruct((m, D_MLSTM), BF16),
            jax.ShapeDtypeStruct(c0.shape, F32),
            jax.ShapeDtypeStruct(n0.shape, F32),
            jax.ShapeDtypeStruct(m_rows.shape, F32),
        ],
        compiler_params=_params(("parallel", "parallel")),
        name="mlstm_sample",
    )(proj, proj, proj, proj, gates, bias, m_rows, c0, n0)


def _ln_swish(y, g, b):
    mu = jnp.mean(y, axis=-1, keepdims=True)
    yc = y - mu
    var = jnp.mean(yc * yc, axis=-1, keepdims=True)
    z = yc * lax.rsqrt(var + EPS) * g + b
    return z * jax.nn.sigmoid(z)


def _conv_prompt_body(a_ref, b_ref, w_ref, bdw_ref, gln_ref, bln_ref, wg_ref, wu_ref, wd_ref,
                      hc_ref, st_ref, wgu_o, wd_o, u_ref, us_ref, wb_ref, y_ref, *, tt):
    t_id = pl.program_id(1)
    n_shift = tt + CONV_PAD - 8

    @pl.when(t_id == 0)
    def _():
        u_ref[0:CONV_PAD, :] = jnp.zeros((CONV_PAD, D_CONV), F32)
        for s in range(CONV_WIDTH):
            wb_ref[s] = jnp.broadcast_to(w_ref[s:s + 1, :], (8, D_CONV))

    for c in range(D_FF // FFN_TF):
        wgu_o[c, :, :FFN_TF] = wg_ref[:, c * FFN_TF:(c + 1) * FFN_TF].astype(BF16)
        wgu_o[c, :, FFN_TF:] = wu_ref[:, c * FFN_TF:(c + 1) * FFN_TF].astype(BF16)
    wd_o[...] = wd_ref[...].astype(BF16)

    u_ref[CONV_PAD:CONV_PAD + tt, :] = a_ref[...] * jax.nn.sigmoid(b_ref[...])
    for r in range(1, 8):
        us_ref[r - 1] = u_ref[r:r + n_shift, :]

    def row_block(i, carry):
        base = pl.multiple_of(i * CONV_ROWS, CONV_ROWS)
        n_slab = CONV_ROWS // 8
        acc = [jnp.broadcast_to(bdw_ref[...], (8, D_CONV))] * n_slab
        for s in range(CONV_WIDTH):
            k8, r = divmod(CONV_PAD - HALO + s, 8)
            w = wb_ref[s]
            for j in range(n_slab):
                rows = pl.ds(pl.multiple_of(base + 8 * (k8 + j), 8), 8)
                win = u_ref[rows, :] if r == 0 else us_ref[r - 1, rows, :]
                acc[j] = acc[j] + w * win
        for j in range(n_slab):
            y_ref[pl.ds(pl.multiple_of(base + 8 * j, 8), 8), :] = acc[j]
        return carry

    lax.fori_loop(0, tt // CONV_ROWS, row_block, 0)
    hc_ref[...] = _ln_swish(y_ref[...], gln_ref[...], bln_ref[...]).astype(BF16)

    @pl.when(t_id == pl.num_programs(1) - 1)
    def _():
        st_ref[0, 0] = u_ref[CONV_PAD + tt - HALO:CONV_PAD + tt, :]

    u_ref[0:CONV_PAD, :] = u_ref[tt:tt + CONV_PAD, :]


def _conv_prompt(proj, w_dw, b_dw, g_ln, b_ln, w_gate, w_up, w_down, batch, seq, *, tt=256):
    nt = seq // tt
    steps = batch * nt
    nf = D_FF // FFN_TF
    r_up, r_dn = D_MODEL // steps, D_FF // steps
    assert r_up * steps == D_MODEL and r_dn * steps == D_FF and r_up % 16 == 0 and r_dn % 16 == 0
    const = lambda b, t: (0, 0)
    slab = lambda b, t: (b * nt + t, 0)
    return pl.pallas_call(
        functools.partial(_conv_prompt_body, tt=tt),
        grid=(batch, nt),
        in_specs=[
            pl.BlockSpec((tt, D_CONV), lambda b, t: (b * nt + t, 4)),
            pl.BlockSpec((tt, D_CONV), lambda b, t: (b * nt + t, 5)),
            pl.BlockSpec((CONV_WIDTH, D_CONV), const),
            pl.BlockSpec((1, D_CONV), const),
            pl.BlockSpec((1, D_CONV), const),
            pl.BlockSpec((1, D_CONV), const),
            pl.BlockSpec((r_up, D_FF), slab),
            pl.BlockSpec((r_up, D_FF), slab),
            pl.BlockSpec((r_dn, D_MODEL), slab),
        ],
        out_specs=[
            pl.BlockSpec((tt, D_CONV), lambda b, t: (b * nt + t, 0)),
            pl.BlockSpec((1, 1, HALO, D_CONV), lambda b, t: (0, b, 0, 0)),
            pl.BlockSpec((nf, r_up, 2 * FFN_TF), lambda b, t: (0, b * nt + t, 0)),
            pl.BlockSpec((r_dn, D_MODEL), slab),
        ],
        out_shape=[
            jax.ShapeDtypeStruct((batch * seq, D_CONV), BF16),
            jax.ShapeDtypeStruct((1, batch, HALO, D_CONV), F32),
            jax.ShapeDtypeStruct((nf, D_MODEL, 2 * FFN_TF), BF16),
            jax.ShapeDtypeStruct((D_FF, D_MODEL), BF16),
        ],
        scratch_shapes=[pltpu.VMEM((tt + CONV_PAD, D_CONV), F32),
                        pltpu.VMEM((7, tt + CONV_PAD - 8, D_CONV), F32),
                        pltpu.VMEM((CONV_WIDTH, 8, D_CONV), F32),
                        pltpu.VMEM((tt, D_CONV), F32)],
        compiler_params=_params(("parallel", "arbitrary")),
        name="conv_prompt",
    )(proj, proj, w_dw, b_dw, g_ln, b_ln, w_gate, w_up, w_down)


def _conv_sample_body(a_ref, b_ref, st_ref, w_ref, bdw_ref, gln_ref, bln_ref, hc_ref, st_out_ref,
                      wb_ref, *, seq):
    @pl.when(pl.program_id(0) == 0)
    def _():
        for s in range(CONV_WIDTH):
            wb_ref[s] = jnp.broadcast_to(w_ref[s:s + 1, :], (8, D_CONV))

    acc = [jnp.broadcast_to(bdw_ref[...], (8, D_CONV))] * seq
    for j in range(HALO + seq):
        slab = st_ref[0, j] if j < HALO else a_ref[j - HALO] * jax.nn.sigmoid(b_ref[j - HALO])
        for t in range(seq):
            if 0 <= j - t < CONV_WIDTH:
                acc[t] = acc[t] + wb_ref[j - t] * slab
        if j >= seq:
            st_out_ref[0, j - seq] = slab
    for t in range(seq):
        hc_ref[t] = _ln_swish(acc[t], gln_ref[...], bln_ref[...])


def _conv_sample(a_t, b_t, state_t, w_dw, b_dw, g_ln, b_ln):
    seq, batch, _ = a_t.shape
    bb = 8
    const = lambda i: (0, 0)
    tok = pl.BlockSpec((seq, bb, D_CONV), lambda i: (0, i, 0))
    hist = pl.BlockSpec((1, HALO, bb, D_CONV), lambda i: (0, 0, i, 0))
    return pl.pallas_call(
        functools.partial(_conv_sample_body, seq=seq),
        grid=(batch // bb,),
        in_specs=[tok, tok, hist,
                  pl.BlockSpec((CONV_WIDTH, D_CONV), const),
                  pl.BlockSpec((1, D_CONV), const),
                  pl.BlockSpec((1, D_CONV), const),
                  pl.BlockSpec((1, D_CONV), const)],
        out_specs=[tok, hist],
        out_shape=[
            jax.ShapeDtypeStruct(a_t.shape, F32),
            jax.ShapeDtypeStruct(state_t.shape, F32),
        ],
        scratch_shapes=[pltpu.VMEM((CONV_WIDTH, 8, D_CONV), F32)],
        compiler_params=_params(("arbitrary",)),
        name="conv_sample",
    )(a_t, b_t, state_t, w_dw, b_dw, g_ln, b_ln)


def _proj_out_body(hm_ref, hc_ref, x_ref, wa_ref, wb_ref, g_ref, o_ref):
    mix = (jnp.dot(hm_ref[...], wa_ref[...], preferred_element_type=F32)
           + jnp.dot(hc_ref[...].astype(BF16), wb_ref[...], preferred_element_type=F32))
    o_ref[...] = x_ref[...] + _rms(mix, g_ref[...])


def _proj_out(hm, hc, x, w_out, g, *, tm=512):
    m = x.shape[0]
    row = lambda i: (i, 0)
    return pl.pallas_call(
        _proj_out_body,
        grid=(m // tm,),
        in_specs=[
            pl.BlockSpec((tm, D_MLSTM), row),
            pl.BlockSpec((tm, D_CONV), row),
            pl.BlockSpec((tm, D_MODEL), row),
            pl.BlockSpec((D_MLSTM, D_MODEL), lambda i: (0, 0)),
            pl.BlockSpec((D_CONV, D_MODEL), lambda i: (1, 0)),
            pl.BlockSpec((1, D_MODEL), lambda i: (0, 0)),
        ],
        out_specs=pl.BlockSpec((tm, D_MODEL), row),
        out_shape=jax.ShapeDtypeStruct((m, D_MODEL), F32),
        compiler_params=_params(("parallel",)),
        name="proj_out",
    )(hm, hc, x, w_out, w_out, g)


def kernel(x_prompt, x_sample, state_mlstm_C, state_mlstm_n, state_mlstm_m, state_conv, g_ffn1_pre, w_ffn1_gate, w_ffn1_up, w_ffn1_down, g_ffn1_post, g_mix_pre, w_in, b_igate, b_fgate, w_dw, b_dw, g_conv_ln, b_conv_ln, w_out, g_mix_post, g_ffn2_pre, w_ffn2_gate, w_ffn2_up, w_ffn2_down, g_ffn2_post):
    depth = state_mlstm_C.shape[0]
    assert depth == 1, "kernel handles a single layer"
    bp, tp, _ = x_prompt.shape
    bs, ts, _ = x_sample.shape
    l = 0

    w_in_t = jnp.swapaxes(w_in, 1, 2)
    w_gate_t = jnp.pad(w_in_t[l, D_MAIN:], ((0, LANES - 2 * N_HEADS), (0, 0))).astype(BF16)
    w_o = w_out[l].astype(BF16)
    bias = jnp.pad(jnp.concatenate([b_igate[l], b_fgate[l]]), (0, LANES - 2 * N_HEADS))[None, :]

    xs, *ffn1 = _ffn(x_sample.reshape(bs * ts, D_MODEL), g_ffn1_pre,
                     (w_ffn1_gate[l], w_ffn1_up[l]), w_ffn1_down[l], g_ffn1_post)
    proj_s, gates_s, w_main_t = _proj_in(xs, g_mix_pre, w_in_t, w_gate_t)
    m_rows = jnp.repeat(state_mlstm_m[l].T, ts, axis=1)[:, :, None]
    hm_s, c_s, n_s, m_s = _mlstm_sample(
        proj_s, gates_s, bias, m_rows, state_mlstm_C, state_mlstm_n[l][:, :, None, :], ts)
    glu_t = jnp.swapaxes(proj_s[:, 4 * D_MLSTM:].reshape(bs, ts, 2, D_CONV), 0, 1)
    hc_t, conv_t = _conv_sample(glu_t[:, :, 0], glu_t[:, :, 1], jnp.swapaxes(state_conv, 1, 2),
                                w_dw[l], b_dw, g_conv_ln, b_conv_ln)
    hc_s = jnp.swapaxes(hc_t, 0, 1).reshape(bs * ts, D_CONV)
    xs = _proj_out(hm_s, hc_s, xs, w_o, g_mix_post)

    xp = _ffn(x_prompt.reshape(bp * tp, D_MODEL), g_ffn1_pre, *ffn1, g_ffn1_post)[0]
    proj_p, gates_p = _proj_in(xp, g_mix_pre, w_main_t, w_gate_t)
    hc_p, conv_p, *ffn2 = _conv_prompt(proj_p, w_dw[l], b_dw, g_conv_ln, b_conv_ln,
                                       w_ffn2_gate[l], w_ffn2_up[l], w_ffn2_down[l], bp, tp)
    xp, c_p, n_p, m_p = _mix_prompt(proj_p, gates_p, bias, hc_p, xp, w_o, g_mix_post, bp, tp)
    ys = _ffn(xs, g_ffn2_pre, *ffn2, g_ffn2_post)[0].reshape(bs, ts, D_MODEL)
    yp = _ffn(xp, g_ffn2_pre, *ffn2, g_ffn2_post)[0].reshape(bp, tp, D_MODEL)

    return (yp, ys,
            c_p[None], n_p[:, :, 0, :][None], m_p[:, :, 0, 0][None], conv_p,
            c_s, n_s[:, :, 0, :][None], m_s[:, ::ts, 0].T[None], jnp.swapaxes(conv_t, 1, 2))
```

```python
import functools

import jax
import jax.numpy as jnp
from jax import lax
from jax.experimental import pallas as pl
from jax.experimental.pallas import tpu as pltpu

D_MODEL = 2048
N_HEADS = 4
HEAD_DIM = 256
D_MLSTM = N_HEADS * HEAD_DIM
D_CONV = D_MODEL - D_MLSTM
CONV_WIDTH = 31
HALO = CONV_WIDTH - 1
CONV_PAD = 32
CONV_ROWS = 32
D_FF = 5632
FFN_TF = 512
D_MAIN = 4 * D_MLSTM + 2 * D_CONV
EPS = 1e-6
FFN_RES = 0.5
K_SCALE = HEAD_DIM ** -0.5

LANES = 128
ROWS = 128
VMEM_LIMIT = 56 * 1024 * 1024

F32 = jnp.float32
BF16 = jnp.bfloat16


def _params(sem):
    return pltpu.CompilerParams(dimension_semantics=sem, vmem_limit_bytes=VMEM_LIMIT)


def _rms(x, g):
    return x * lax.rsqrt(jnp.mean(x * x, axis=-1, keepdims=True) + EPS) * g


def _ffn_cast_body(x_ref, gpre_ref, wg_ref, wu_ref, wd_ref, gpost_ref, o_ref, wgu_o, wd_o,
                   h_ref):
    f = pl.program_id(1)

    @pl.when(f == 0)
    def _():
        h_ref[...] = _rms(x_ref[...], gpre_ref[...]).astype(BF16)
        o_ref[...] = jnp.zeros_like(o_ref)

    tf = wg_ref.shape[1]
    wgu_o[0, :, :tf] = wg_ref[...].astype(BF16)
    wgu_o[0, :, tf:] = wu_ref[...].astype(BF16)
    wd_o[...] = wd_ref[...].astype(BF16)
    o_ref[...] += jnp.dot(_swiglu_act(h_ref[...], wgu_o[0]), wd_o[...], preferred_element_type=F32)

    @pl.when(f == pl.num_programs(1) - 1)
    def _():
        o_ref[...] = x_ref[...] + FFN_RES * _rms(o_ref[...], gpost_ref[...])


def _swiglu_act(h, w_gate_up):
    gu = jnp.dot(h, w_gate_up, preferred_element_type=F32)
    tf = gu.shape[1] // 2
    g, u = gu[:, :tf], gu[:, tf:]
    return ((g * jax.nn.sigmoid(g)) * u).astype(BF16)


def _ffn_skew_body(x_ref, gpre_ref, wgu_ref, wd_ref, gpost_ref, o_ref,
                   h_ref, a0_ref, a1_ref, *, nf):
    f = pl.program_id(1)

    def gate_up(a_ref):
        a_ref[...] = _swiglu_act(h_ref[...], wgu_ref[0])

    def down(a_ref):
        o_ref[...] += jnp.dot(a_ref[...], wd_ref[...], preferred_element_type=F32)

    @pl.when(f == 0)
    def _():
        h_ref[...] = _rms(x_ref[...], gpre_ref[...]).astype(BF16)
        o_ref[...] = jnp.zeros_like(o_ref)
        gate_up(a0_ref)

    @pl.when((f > 0) & (f < nf) & ((f & 1) == 1))
    def _():
        gate_up(a1_ref)
        down(a0_ref)

    @pl.when((f > 0) & (f < nf) & ((f & 1) == 0))
    def _():
        gate_up(a0_ref)
        down(a1_ref)

    @pl.when(f == nf)
    def _():
        down(a0_ref if (nf - 1) % 2 == 0 else a1_ref)
        o_ref[...] = x_ref[...] + FFN_RES * _rms(o_ref[...], gpost_ref[...])


def _ffn(x, gpre, w_up, wd, gpost):
    m = x.shape[0]
    cast = wd.dtype == F32
    tm = min(m, 512 if cast else 1024)
    nf = D_FF // FFN_TF
    row = lambda i, f: (i, 0)
    const = lambda i, f: (0, 0)
    out_specs = [pl.BlockSpec((tm, D_MODEL), row)]
    out_shape = [jax.ShapeDtypeStruct((m, D_MODEL), F32)]
    if cast:
        assert m == tm, "the casting variant writes each weight tile once"
        tf = FFN_TF
        body = _ffn_cast_body
        steps = nf
        up_spec = pl.BlockSpec((D_MODEL, tf), lambda i, f: (0, f))
        up_specs = [up_spec, up_spec]
        up_args = list(w_up)
        down_spec = pl.BlockSpec((tf, D_MODEL), lambda i, f: (f, 0))
        out_specs += [pl.BlockSpec((1, D_MODEL, 2 * tf), lambda i, f: (f, 0, 0)), down_spec]
        out_shape += [jax.ShapeDtypeStruct((nf, D_MODEL, 2 * tf), BF16), jax.ShapeDtypeStruct(wd.shape, BF16)]
        scratch = []
    else:
        tf = FFN_TF
        body = functools.partial(_ffn_skew_body, nf=nf)
        steps = nf + 1
        up_specs = [pl.BlockSpec((1, D_MODEL, 2 * tf), lambda i, f: (jnp.minimum(f, nf - 1), 0, 0))]
        up_args = [w_up]
        down_spec = pl.BlockSpec((tf, D_MODEL), lambda i, f: (jnp.maximum(f - 1, 0), 0))
        scratch = [pltpu.VMEM((tm, tf), BF16), pltpu.VMEM((tm, tf), BF16)]
    return pl.pallas_call(
        body,
        grid=(m // tm, steps),
        in_specs=[pl.BlockSpec((tm, D_MODEL), row), pl.BlockSpec((1, D_MODEL), const),
                  *up_specs, down_spec, pl.BlockSpec((1, D_MODEL), const)],
        out_specs=out_specs,
        out_shape=out_shape,
        scratch_shapes=[pltpu.VMEM((tm, D_MODEL), BF16)] + scratch,
        compiler_params=_params(("parallel", "arbitrary")),
        name="ffn_cast" if cast else "ffn",
    )(x, gpre, *up_args, wd, gpost)


def _proj_in_body(*refs, cast):
    if cast:
        x_ref, g_ref, w_ref, wgate_ref, proj_ref, gates_ref, w_o, h_ref = refs
    else:
        x_ref, g_ref, w_ref, wgate_ref, proj_ref, gates_ref, h_ref = refs
    nt = (((1,), (1,)), ((), ()))

    @pl.when(pl.program_id(1) == 0)
    def _():
        h = _rms(x_ref[...], g_ref[...]).astype(BF16)
        h_ref[...] = h
        gates_ref[...] = lax.dot_general(h, wgate_ref[...], nt, preferred_element_type=F32)

    if cast:
        w_o[...] = w_ref[0].astype(BF16)
        w_ref = w_o
    proj_ref[...] = lax.dot_general(h_ref[...], w_ref[...], nt, preferred_element_type=F32)


def _proj_in(x, g, w_main_t, w_gate_t):
    m = x.shape[0]
    cast = w_main_t.dtype == F32
    tm = min(m, 1024)
    tn = 1024 if cast else 1536
    if cast:
        assert m == tm, "the casting variant writes each weight tile once"
        w_spec = pl.BlockSpec((1, tn, D_MODEL), lambda i, n: (0, n, 0))
    else:
        w_spec = pl.BlockSpec((tn, D_MODEL), lambda i, n: (n, 0))
    out_specs = [pl.BlockSpec((tm, tn), lambda i, n: (i, n)), pl.BlockSpec((tm, LANES), lambda i, n: (i, 0))]
    out_shape = [jax.ShapeDtypeStruct((m, D_MAIN), F32), jax.ShapeDtypeStruct((m, LANES), F32)]
    if cast:
        out_specs.append(pl.BlockSpec((tn, D_MODEL), lambda i, n: (n, 0)))
        out_shape.append(jax.ShapeDtypeStruct((D_MAIN, D_MODEL), BF16))
    return pl.pallas_call(
        functools.partial(_proj_in_body, cast=cast),
        grid=(m // tm, D_MAIN // tn),
        in_specs=[
            pl.BlockSpec((tm, D_MODEL), lambda i, n: (i, 0)),
            pl.BlockSpec((1, D_MODEL), lambda i, n: (0, 0)),
            w_spec,
            pl.BlockSpec((LANES, D_MODEL), lambda i, n: (0, 0)),
        ],
        out_specs=out_specs,
        out_shape=out_shape,
        scratch_shapes=[pltpu.VMEM((tm, D_MODEL), BF16)],
        compiler_params=_params(("parallel", "arbitrary")),
        name="proj_in_cast" if cast else "proj_in",
    )(x, g, w_main_t, w_gate_t)


def _log_sigmoid(x):
    return jnp.minimum(x, 0.0) - jnp.log1p(jnp.exp(-jnp.abs(x)))


def _seg_cumsum(x, seg_len):
    pos = lax.broadcasted_iota(jnp.int32, x.shape, 0) & (seg_len - 1)
    shift = 1
    while shift < seg_len:
        x = x + jnp.where(pos >= shift, pltpu.roll(x, shift, 0), 0.0)
        shift *= 2
    return x


def _gate_terms(pre, seg_len):
    bt = _seg_cumsum(_log_sigmoid(pre), seg_len)
    return pre, bt, pre.T, bt.T


def _tile_masks(seg_len):
    log2 = seg_len.bit_length() - 1
    t_idx = lax.broadcasted_iota(jnp.int32, (ROWS, ROWS), 0)
    s_idx = lax.broadcasted_iota(jnp.int32, (ROWS, ROWS), 1)
    if seg_len == ROWS:
        return s_idx <= t_idx, None, None
    same = (t_idx >> log2) == (s_idx >> log2)
    last = s_idx == ((t_idx >> log2) << log2) + (seg_len - 1)
    return same & (s_idx <= t_idx), same, last


def _pick(x, j, axis):
    if isinstance(j, int):
        return x[:, j:j + 1] if axis == 1 else x[j:j + 1, :]
    idx = lax.broadcasted_iota(jnp.int32, x.shape, axis)
    return jnp.sum(jnp.where(idx == j, x, 0.0), axis=axis, keepdims=True)


def _mlstm_tile(q, k, v, gates, masks, hd, seg_len, m_prev):
    pre, bt_all, pre_t, bt_t = gates
    valid, same, last = masks
    ig_col = _pick(pre, hd, 1)
    bt_col = _pick(bt_all, hd + N_HEADS, 1)
    key_w = _pick(pre_t, hd, 0) - _pick(bt_t, hd + N_HEADS, 0)

    d = jnp.where(valid, bt_col + key_w, -jnp.inf)
    inter = bt_col + m_prev
    m_t = jnp.maximum(inter, jnp.max(d, axis=1, keepdims=True))
    w_intra = jnp.exp(d - m_t)
    w_inter = jnp.exp(inter - m_t)

    if seg_len == ROWS:
        bt_last = bt_col[ROWS - 1:ROWS, :]
        m_end = m_t[ROWS - 1:ROWS, :]
    else:
        bt_row = _pick(bt_t, hd + N_HEADS, 0)
        bt_last = jnp.sum(jnp.where(last, bt_row, 0.0), axis=1, keepdims=True)
        e = jnp.where(same, bt_last + key_w, -jnp.inf)
        m_end = jnp.maximum(bt_last + m_prev, jnp.max(e, axis=1, keepdims=True))
    g_keys = jnp.exp(bt_last - bt_col + ig_col - m_end)
    g_state = jnp.exp(bt_last + m_prev - m_end)

    ks = k * K_SCALE
    q_bf = q.astype(BF16)
    v_bf = v.astype(BF16)
    s = lax.dot_general(q_bf, ks.astype(BF16), (((1,), (1,)), ((), ())),
                        preferred_element_type=F32) * w_intra
    num = jnp.dot(s.astype(BF16), v_bf, preferred_element_type=F32)
    den = jnp.sum(s, axis=1, keepdims=True)
    kg = ks * g_keys
    return dict(q_bf=q_bf, v_bf=v_bf, num=num, den=den, kg=kg, m_t=m_t, w_inter=w_inter,
                g_state=g_state, m_end=m_end)


def _mlstm_out(t, q, o, q_c, q_n):
    num = t["num"] + q_c * t["w_inter"]
    den = t["den"] + q_n * t["w_inter"]
    h = num / jnp.maximum(jnp.abs(den), jnp.exp(-t["m_t"]))
    return (jax.nn.sigmoid(o) * h).astype(BF16)


def _mix_prompt_body(q_ref, k_ref, v_ref, o_ref, gates_ref, bias_ref, hc_ref, x_ref, wa_ref, wb_ref, g_ref,
                     y_ref, c_out_ref, n_out_ref, m_out_ref, c_ref, n_ref, m_ref, hm_ref, *, n_chunks, nt):
    t = pl.program_id(1)

    def project():
        mix = (jnp.dot(hm_ref[(t - 1) & 1], wa_ref[...], preferred_element_type=F32)
               + jnp.dot(hc_ref[...], wb_ref[...], preferred_element_type=F32))
        y_ref[...] = x_ref[...] + _rms(mix, g_ref[...])

    def recur():
        slot = t & 1
        masks = _tile_masks(ROWS)
        for c in range(n_chunks):
            rows = pl.ds(c * ROWS, ROWS)
            gates = _gate_terms(gates_ref[rows, :] + bias_ref[...], ROWS)
            for hd in range(N_HEADS):
                cols = pl.ds(hd * HEAD_DIM, HEAD_DIM)
                q = q_ref[rows, cols]
                tl = _mlstm_tile(q, k_ref[rows, cols], v_ref[rows, cols], gates, masks, hd, ROWS,
                                 m_ref[hd][:, 0:1])
                c_old = c_ref[hd]
                n_old = n_ref[hd]
                q_c = jnp.dot(tl["q_bf"], c_old.astype(BF16), preferred_element_type=F32)
                q_n = jnp.sum(q * n_old, axis=1, keepdims=True)
                hm_ref[slot, rows, cols] = _mlstm_out(tl, q, o_ref[rows, cols], q_c, q_n)
                g = tl["g_state"]
                c_ref[hd] = g * c_old + lax.dot_general(
                    tl["kg"].astype(BF16), tl["v_bf"], (((0,), (0,)), ((), ())), preferred_element_type=F32)
                n_ref[hd] = g * n_old + jnp.sum(tl["kg"], axis=0, keepdims=True)
                m_ref[hd] = jnp.broadcast_to(tl["m_end"], (1, LANES))

    @pl.when(t == 0)
    def _():
        c_ref[...] = jnp.zeros_like(c_ref)
        n_ref[...] = jnp.zeros_like(n_ref)
        m_ref[...] = jnp.zeros_like(m_ref)
        recur()

    @pl.when((t > 0) & (t < nt))
    def _():
        project()
        recur()

    @pl.when(t == nt)
    def _():
        project()
        c_out_ref[0] = c_ref[...]
        n_out_ref[0] = n_ref[...]
        m_out_ref[0] = m_ref[...]


def _mix_prompt(proj, gates, bias, hc, x, w_out, g, batch, seq, *, tt=256):
    nt = seq // tt
    cur = lambda j: (lambda b, t: (b * nt + jnp.minimum(t, nt - 1), j))
    prev = lambda b, t: (b * nt + jnp.maximum(t - 1, 0), 0)
    state = lambda b, t: (b, 0, 0, 0)
    const = lambda b, t: (0, 0)
    return pl.pallas_call(
        functools.partial(_mix_prompt_body, n_chunks=tt // ROWS, nt=nt),
        grid=(batch, nt + 1),
        in_specs=[
            pl.BlockSpec((tt, D_MLSTM), cur(0)),
            pl.BlockSpec((tt, D_MLSTM), cur(1)),
            pl.BlockSpec((tt, D_MLSTM), cur(2)),
            pl.BlockSpec((tt, D_MLSTM), cur(3)),
            pl.BlockSpec((tt, LANES), cur(0)),
            pl.BlockSpec((1, LANES), const),
            pl.BlockSpec((tt, D_CONV), prev),
            pl.BlockSpec((tt, D_MODEL), prev),
            pl.BlockSpec((D_MLSTM, D_MODEL), lambda b, t: (0, 0)),
            pl.BlockSpec((D_CONV, D_MODEL), lambda b, t: (1, 0)),
            pl.BlockSpec((1, D_MODEL), const),
        ],
        out_specs=[
            pl.BlockSpec((tt, D_MODEL), prev),
            pl.BlockSpec((1, N_HEADS, HEAD_DIM, HEAD_DIM), state),
            pl.BlockSpec((1, N_HEADS, 1, HEAD_DIM), state),
            pl.BlockSpec((1, N_HEADS, 1, LANES), state),
        ],
        out_shape=[
            jax.ShapeDtypeStruct((batch * seq, D_MODEL), F32),
            jax.ShapeDtypeStruct((batch, N_HEADS, HEAD_DIM, HEAD_DIM), F32),
            jax.ShapeDtypeStruct((batch, N_HEADS, 1, HEAD_DIM), F32),
            jax.ShapeDtypeStruct((batch, N_HEADS, 1, LANES), F32),
        ],
        scratch_shapes=[pltpu.VMEM((N_HEADS, HEAD_DIM, HEAD_DIM), F32), pltpu.VMEM((N_HEADS, 1, HEAD_DIM), F32),
                        pltpu.VMEM((N_HEADS, 1, LANES), F32), pltpu.VMEM((2, tt, D_MLSTM), BF16)],
        compiler_params=_params(("parallel", "arbitrary")),
        name="mix_prompt",
    )(proj, proj, proj, proj, gates, bias, hc, x, w_out, w_out, g)


def _mlstm_sample_body(q_ref, k_ref, v_ref, o_ref, gates_ref, bias_ref, mrow_ref, c_ref, n_ref,
                       hm_ref, c_out_ref, n_out_ref, m_out_ref, *, seg_len):
    hd = pl.program_id(1)
    n_seg = ROWS // seg_len
    grp = 16 // seg_len
    log2 = seg_len.bit_length() - 1
    q = q_ref[...]
    gates = _gate_terms(gates_ref[...] + bias_ref[...], seg_len)
    t = _mlstm_tile(q, k_ref[...], v_ref[...], gates, _tile_masks(seg_len), hd, seg_len, mrow_ref[0])

    seg_of_row = lax.broadcasted_iota(jnp.int32, (16, 1), 0) >> log2
    qc_parts, n_parts = [], []
    for j in range(ROWS // 16):
        qg = t["q_bf"][16 * j:16 * (j + 1)]
        qc, nr = None, None
        for i in range(grp):
            b = grp * j + i
            r = jnp.dot(qg, c_ref[0, b, 0].astype(BF16), preferred_element_type=F32)
            nb = jnp.broadcast_to(n_ref[b, 0], (16, HEAD_DIM))
            qc = r if i == 0 else jnp.where(seg_of_row == i, r, qc)
            nr = nb if i == 0 else jnp.where(seg_of_row == i, nb, nr)
        qc_parts.append(qc)
        n_parts.append(nr)
    q_c = jnp.concatenate(qc_parts, axis=0)
    q_n = jnp.sum(q * jnp.concatenate(n_parts, axis=0), axis=1, keepdims=True)
    hm_ref[...] = _mlstm_out(t, q, o_ref[...], q_c, q_n)

    kg = t["kg"]
    kg_t = kg.T
    seg_of_lane = lax.broadcasted_iota(jnp.int32, (1, ROWS), 1) >> log2
    seg_of_row8 = lax.broadcasted_iota(jnp.int32, (8, 1), 0) >> log2
    per8 = 8 // seg_len
    for b in range(n_seg):
        g = t["g_state"][seg_len * b:seg_len * b + 1, :]
        upd = jnp.dot(jnp.where(seg_of_lane == b, kg_t, 0.0).astype(BF16), t["v_bf"],
                      preferred_element_type=F32)
        c_out_ref[0, b, 0] = g * c_ref[0, b, 0] + upd
        kg8 = kg[8 * (b // per8):8 * (b // per8) + 8]
        n_out_ref[b, 0] = g * n_ref[b, 0] + jnp.sum(
            jnp.where(seg_of_row8 == (b % per8), kg8, 0.0), axis=0, keepdims=True)
    m_out_ref[0] = t["m_end"]


def _mlstm_sample(proj, gates, bias, m_rows, c0, n0, seg_len):
    m = proj.shape[0]
    n_seg = ROWS // seg_len
    col = lambda j: (lambda i, h: (i, j * N_HEADS + h))
    return pl.pallas_call(
        functools.partial(_mlstm_sample_body, seg_len=seg_len),
        grid=(m // ROWS, N_HEADS),
        in_specs=[
            pl.BlockSpec((ROWS, HEAD_DIM), col(0)),
            pl.BlockSpec((ROWS, HEAD_DIM), col(1)),
            pl.BlockSpec((ROWS, HEAD_DIM), col(2)),
            pl.BlockSpec((ROWS, HEAD_DIM), col(3)),
            pl.BlockSpec((ROWS, LANES), lambda i, h: (i, 0)),
            pl.BlockSpec((1, LANES), lambda i, h: (0, 0)),
            pl.BlockSpec((1, ROWS, 1), lambda i, h: (h, i, 0)),
            pl.BlockSpec((1, n_seg, 1, HEAD_DIM, HEAD_DIM), lambda i, h: (0, i, h, 0, 0)),
            pl.BlockSpec((n_seg, 1, 1, HEAD_DIM), lambda i, h: (i, h, 0, 0)),
        ],
        out_specs=[
            pl.BlockSpec((ROWS, HEAD_DIM), lambda i, h: (i, h)),
            pl.BlockSpec((1, n_seg, 1, HEAD_DIM, HEAD_DIM), lambda i, h: (0, i, h, 0, 0)),
            pl.BlockSpec((n_seg, 1, 1, HEAD_DIM), lambda i, h: (i, h, 0, 0)),
            pl.BlockSpec((1, ROWS, 1), lambda i, h: (h, i, 0)),
        ],
        out_shape=[
            jax.ShapeDtypeStruct((m, D_MLSTM), BF16),
            jax.ShapeDtypeStruct(c0.shape, F32),
            jax.ShapeDtypeStruct(n0.shape, F32),
            jax.ShapeDtypeStruct(m_rows.shape, F32),
        ],
        compiler_params=_params(("parallel", "parallel")),
        name="mlstm_sample",
    )(proj, proj, proj, proj, gates, bias, m_rows, c0, n0)


def _ln_swish(y, g, b):
    mu = jnp.mean(y, axis=-1, keepdims=True)
    yc = y - mu
    var = jnp.mean(yc * yc, axis=-1, keepdims=True)
    z = yc * lax.rsqrt(var + EPS) * g + b
    return z * jax.nn.sigmoid(z)


def _conv_prompt_body(a_ref, b_ref, w_ref, bdw_ref, gln_ref, bln_ref, wg_ref, wu_ref, wd_ref, wo_ref,
                      hc_ref, st_ref, wgu_o, wd_o, wo_o, u_ref, us_ref, wb_ref, y_ref, *, tt):
    t_id = pl.program_id(1)
    n_shift = tt + CONV_PAD - 8

    @pl.when(t_id == 0)
    def _():
        u_ref[0:CONV_PAD, :] = jnp.zeros((CONV_PAD, D_CONV), F32)
        for s in range(CONV_WIDTH):
            wb_ref[s] = jnp.broadcast_to(w_ref[s:s + 1, :], (8, D_CONV))

    for c in range(D_FF // FFN_TF):
        wgu_o[c, :, :FFN_TF] = wg_ref[:, c * FFN_TF:(c + 1) * FFN_TF].astype(BF16)
        wgu_o[c, :, FFN_TF:] = wu_ref[:, c * FFN_TF:(c + 1) * FFN_TF].astype(BF16)
    wd_o[...] = wd_ref[...].astype(BF16)
    wo_o[...] = wo_ref[...].astype(BF16)

    u_ref[CONV_PAD:CONV_PAD + tt, :] = a_ref[...] * jax.nn.sigmoid(b_ref[...])
    for r in range(1, 8):
        us_ref[r - 1] = u_ref[r:r + n_shift, :]

    def row_block(i, carry):
        base = pl.multiple_of(i * CONV_ROWS, CONV_ROWS)
        n_slab = CONV_ROWS // 8
        acc = [jnp.broadcast_to(bdw_ref[...], (8, D_CONV))] * n_slab
        for s in range(CONV_WIDTH):
            k8, r = divmod(CONV_PAD - HALO + s, 8)
            w = wb_ref[s]
            for j in range(n_slab):
                rows = pl.ds(pl.multiple_of(base + 8 * (k8 + j), 8), 8)
                win = u_ref[rows, :] if r == 0 else us_ref[r - 1, rows, :]
                acc[j] = acc[j] + w * win
        for j in range(n_slab):
            y_ref[pl.ds(pl.multiple_of(base + 8 * j, 8), 8), :] = acc[j]
        return carry

    lax.fori_loop(0, tt // CONV_ROWS, row_block, 0)
    hc_ref[...] = _ln_swish(y_ref[...], gln_ref[...], bln_ref[...]).astype(BF16)

    @pl.when(t_id == pl.num_programs(1) - 1)
    def _():
        st_ref[0, 0] = u_ref[CONV_PAD + tt - HALO:CONV_PAD + tt, :]

    u_ref[0:CONV_PAD, :] = u_ref[tt:tt + CONV_PAD, :]


def _conv_prompt(proj, w_dw, b_dw, g_ln, b_ln, w_gate, w_up, w_down, w_out, batch, seq, *, tt=256):
    nt = seq // tt
    steps = batch * nt
    nf = D_FF // FFN_TF
    r_up, r_dn = D_MODEL // steps, D_FF // steps
    assert r_up * steps == D_MODEL and r_dn * steps == D_FF and r_up % 16 == 0 and r_dn % 16 == 0
    assert w_out.shape == (D_MODEL, D_MODEL)
    const = lambda b, t: (0, 0)
    slab = lambda b, t: (b * nt + t, 0)
    return pl.pallas_call(
        functools.partial(_conv_prompt_body, tt=tt),
        grid=(batch, nt),
        in_specs=[
            pl.BlockSpec((tt, D_CONV), lambda b, t: (b * nt + t, 4)),
            pl.BlockSpec((tt, D_CONV), lambda b, t: (b * nt + t, 5)),
            pl.BlockSpec((CONV_WIDTH, D_CONV), const),
            pl.BlockSpec((1, D_CONV), const),
            pl.BlockSpec((1, D_CONV), const),
            pl.BlockSpec((1, D_CONV), const),
            pl.BlockSpec((r_up, D_FF), slab),
            pl.BlockSpec((r_up, D_FF), slab),
            pl.BlockSpec((r_dn, D_MODEL), slab),
            pl.BlockSpec((r_up, D_MODEL), slab),
        ],
        out_specs=[
            pl.BlockSpec((tt, D_CONV), lambda b, t: (b * nt + t, 0)),
            pl.BlockSpec((1, 1, HALO, D_CONV), lambda b, t: (0, b, 0, 0)),
            pl.BlockSpec((nf, r_up, 2 * FFN_TF), lambda b, t: (0, b * nt + t, 0)),
            pl.BlockSpec((r_dn, D_MODEL), slab),
            pl.BlockSpec((r_up, D_MODEL), slab),
        ],
        out_shape=[
            jax.ShapeDtypeStruct((batch * seq, D_CONV), BF16),
            jax.ShapeDtypeStruct((1, batch, HALO, D_CONV), F32),
            jax.ShapeDtypeStruct((nf, D_MODEL, 2 * FFN_TF), BF16),
            jax.ShapeDtypeStruct((D_FF, D_MODEL), BF16),
            jax.ShapeDtypeStruct((D_MODEL, D_MODEL), BF16),
        ],
        scratch_shapes=[pltpu.VMEM((tt + CONV_PAD, D_CONV), F32),
                        pltpu.VMEM((7, tt + CONV_PAD - 8, D_CONV), F32),
                        pltpu.VMEM((CONV_WIDTH, 8, D_CONV), F32),
                        pltpu.VMEM((tt, D_CONV), F32)],
        compiler_params=_params(("parallel", "arbitrary")),
        name="conv_prompt",
    )(proj, proj, w_dw, b_dw, g_ln, b_ln, w_gate, w_up, w_down, w_out)


def _conv_sample_body(a_ref, b_ref, st_ref, w_ref, bdw_ref, gln_ref, bln_ref, hc_ref, st_out_ref,
                      wb_ref, *, seq):
    @pl.when(pl.program_id(0) == 0)
    def _():
        for s in range(CONV_WIDTH):
            wb_ref[s] = jnp.broadcast_to(w_ref[s:s + 1, :], (8, D_CONV))

    acc = [jnp.broadcast_to(bdw_ref[...], (8, D_CONV))] * seq
    for j in range(HALO + seq):
        slab = st_ref[0, j] if j < HALO else a_ref[j - HALO] * jax.nn.sigmoid(b_ref[j - HALO])
        for t in range(seq):
            if 0 <= j - t < CONV_WIDTH:
                acc[t] = acc[t] + wb_ref[j - t] * slab
        if j >= seq:
            st_out_ref[0, j - seq] = slab
    for t in range(seq):
        hc_ref[t] = _ln_swish(acc[t], gln_ref[...], bln_ref[...])


def _conv_sample(a_t, b_t, state_t, w_dw, b_dw, g_ln, b_ln):
    seq, batch, _ = a_t.shape
    bb = 8
    const = lambda i: (0, 0)
    tok = pl.BlockSpec((seq, bb, D_CONV), lambda i: (0, i, 0))
    hist = pl.BlockSpec((1, HALO, bb, D_CONV), lambda i: (0, 0, i, 0))
    return pl.pallas_call(
        functools.partial(_conv_sample_body, seq=seq),
        grid=(batch // bb,),
        in_specs=[tok, tok, hist,
                  pl.BlockSpec((CONV_WIDTH, D_CONV), const),
                  pl.BlockSpec((1, D_CONV), const),
                  pl.BlockSpec((1, D_CONV), const),
                  pl.BlockSpec((1, D_CONV), const)],
        out_specs=[tok, hist],
        out_shape=[
            jax.ShapeDtypeStruct(a_t.shape, F32),
            jax.ShapeDtypeStruct(state_t.shape, F32),
        ],
        scratch_shapes=[pltpu.VMEM((CONV_WIDTH, 8, D_CONV), F32)],
        compiler_params=_params(("arbitrary",)),
        name="conv_sample",
    )(a_t, b_t, state_t, w_dw, b_dw, g_ln, b_ln)


def _proj_out_body(hm_ref, hc_ref, x_ref, wa_ref, wb_ref, g_ref, o_ref):
    mix = (jnp.dot(hm_ref[...], wa_ref[...], preferred_element_type=F32)
           + jnp.dot(hc_ref[...].astype(BF16), wb_ref[...], preferred_element_type=F32))
    o_ref[...] = x_ref[...] + _rms(mix, g_ref[...])


def _proj_out(hm, hc, x, w_out, g, *, tm=512):
    m = x.shape[0]
    row = lambda i: (i, 0)
    return pl.pallas_call(
        _proj_out_body,
        grid=(m // tm,),
        in_specs=[
            pl.BlockSpec((tm, D_MLSTM), row),
            pl.BlockSpec((tm, D_CONV), row),
            pl.BlockSpec((tm, D_MODEL), row),
            pl.BlockSpec((D_MLSTM, D_MODEL), lambda i: (0, 0)),
            pl.BlockSpec((D_CONV, D_MODEL), lambda i: (1, 0)),
            pl.BlockSpec((1, D_MODEL), lambda i: (0, 0)),
        ],
        out_specs=pl.BlockSpec((tm, D_MODEL), row),
        out_shape=jax.ShapeDtypeStruct((m, D_MODEL), F32),
        compiler_params=_params(("parallel",)),
        name="proj_out",
    )(hm, hc, x, w_out, w_out, g)


def kernel(x_prompt, x_sample, state_mlstm_C, state_mlstm_n, state_mlstm_m, state_conv, g_ffn1_pre, w_ffn1_gate, w_ffn1_up, w_ffn1_down, g_ffn1_post, g_mix_pre, w_in, b_igate, b_fgate, w_dw, b_dw, g_conv_ln, b_conv_ln, w_out, g_mix_post, g_ffn2_pre, w_ffn2_gate, w_ffn2_up, w_ffn2_down, g_ffn2_post):
    depth = state_mlstm_C.shape[0]
    assert depth == 1, "kernel handles a single layer"
    bp, tp, _ = x_prompt.shape
    bs, ts, _ = x_sample.shape
    l = 0

    w_in_t = jnp.swapaxes(w_in, 1, 2)
    w_gate_t = jnp.pad(w_in_t[l, D_MAIN:], ((0, LANES - 2 * N_HEADS), (0, 0))).astype(BF16)
    bias = jnp.pad(jnp.concatenate([b_igate[l], b_fgate[l]]), (0, LANES - 2 * N_HEADS))[None, :]

    xs, *ffn1 = _ffn(x_sample.reshape(bs * ts, D_MODEL), g_ffn1_pre,
                     (w_ffn1_gate[l], w_ffn1_up[l]), w_ffn1_down[l], g_ffn1_post)
    proj_s, gates_s, w_main_t = _proj_in(xs, g_mix_pre, w_in_t, w_gate_t)
    xp = _ffn(x_prompt.reshape(bp * tp, D_MODEL), g_ffn1_pre, *ffn1, g_ffn1_post)[0]
    proj_p, gates_p = _proj_in(xp, g_mix_pre, w_main_t, w_gate_t)
    hc_p, conv_p, *ffn2, w_o = _conv_prompt(proj_p, w_dw[l], b_dw, g_conv_ln, b_conv_ln,
                                            w_ffn2_gate[l], w_ffn2_up[l], w_ffn2_down[l], w_out[l], bp, tp)

    m_rows = jnp.repeat(state_mlstm_m[l].T, ts, axis=1)[:, :, None]
    hm_s, c_s, n_s, m_s = _mlstm_sample(
        proj_s, gates_s, bias, m_rows, state_mlstm_C, state_mlstm_n[l][:, :, None, :], ts)
    glu_t = jnp.swapaxes(proj_s[:, 4 * D_MLSTM:].reshape(bs, ts, 2, D_CONV), 0, 1)
    hc_t, conv_t = _conv_sample(glu_t[:, :, 0], glu_t[:, :, 1], jnp.swapaxes(state_conv, 1, 2),
                                w_dw[l], b_dw, g_conv_ln, b_conv_ln)
    hc_s = jnp.swapaxes(hc_t, 0, 1).reshape(bs * ts, D_CONV)
    xs = _proj_out(hm_s, hc_s, xs, w_o, g_mix_post)
    ys = _ffn(xs, g_ffn2_pre, *ffn2, g_ffn2_post)[0].reshape(bs, ts, D_MODEL)

    xp, c_p, n_p, m_p = _mix_prompt(proj_p, gates_p, bias, hc_p, xp, w_o, g_mix_post, bp, tp)
    yp = _ffn(xp, g_ffn2_pre, *ffn2, g_ffn2_post)[0].reshape(bp, tp, D_MODEL)

    return (yp, ys,
            c_p[None], n_p[:, :, 0, :][None], m_p[:, :, 0, 0][None], conv_p,
            c_s, n_s[:, :, 0, :][None], m_s[:, ::ts, 0].T[None], jnp.swapaxes(conv_t, 1, 2))
```

```python
import functools

import jax
import jax.numpy as jnp
from jax import lax
from jax.experimental import pallas as pl
from jax.experimental.pallas import tpu as pltpu

D_MODEL = 2048
N_HEADS = 4
HEAD_DIM = 256
D_MLSTM = N_HEADS * HEAD_DIM
D_CONV = D_MODEL - D_MLSTM
CONV_WIDTH = 31
HALO = CONV_WIDTH - 1
CONV_PAD = 32
CONV_ROWS = 32
D_FF = 5632
FFN_TF = 512
D_MAIN = 4 * D_MLSTM + 2 * D_CONV
EPS = 1e-6
FFN_RES = 0.5
K_SCALE = HEAD_DIM ** -0.5

LANES = 128
ROWS = 128
VMEM_LIMIT = 56 * 1024 * 1024

F32 = jnp.float32
BF16 = jnp.bfloat16


def _params(sem, vmem_limit=VMEM_LIMIT):
    return pltpu.CompilerParams(dimension_semantics=sem, vmem_limit_bytes=vmem_limit)


def _rms(x, g):
    return x * lax.rsqrt(jnp.mean(x * x, axis=-1, keepdims=True) + EPS) * g


def _ffn_cast_body(x_ref, gpre_ref, wg_ref, wu_ref, wd_ref, gpost_ref, o_ref, wgu_o, wd_o,
                   h_ref):
    f = pl.program_id(1)

    @pl.when(f == 0)
    def _():
        h_ref[...] = _rms(x_ref[...], gpre_ref[...]).astype(BF16)
        o_ref[...] = jnp.zeros_like(o_ref)

    tf = wg_ref.shape[1]
    wgu_o[0, :, :tf] = wg_ref[...].astype(BF16)
    wgu_o[0, :, tf:] = wu_ref[...].astype(BF16)
    wd_o[...] = wd_ref[...].astype(BF16)
    o_ref[...] += jnp.dot(_swiglu_act(h_ref[...], wgu_o[0]), wd_o[...], preferred_element_type=F32)

    @pl.when(f == pl.num_programs(1) - 1)
    def _():
        o_ref[...] = x_ref[...] + FFN_RES * _rms(o_ref[...], gpost_ref[...])


def _swiglu_act(h, w_gate_up):
    gu = jnp.dot(h, w_gate_up, preferred_element_type=F32)
    tf = gu.shape[1] // 2
    g, u = gu[:, :tf], gu[:, tf:]
    return ((g * jax.nn.sigmoid(g)) * u).astype(BF16)


def _ffn_skew_body(*refs, nf, side):
    if side:
        x_ref, gpre_ref, wgu_ref, wd_ref, gpost_ref, side_ref, o_ref, side_o, h_ref, a0_ref, a1_ref = refs
    else:
        x_ref, gpre_ref, wgu_ref, wd_ref, gpost_ref, o_ref, h_ref, a0_ref, a1_ref = refs
    f = pl.program_id(1)

    def gate_up(a_ref):
        a_ref[...] = _swiglu_act(h_ref[...], wgu_ref[0])

    def cast_side():
        if side:
            side_o[...] = side_ref[0].astype(BF16)

    def down(a_ref):
        cast_side()
        o_ref[...] += jnp.dot(a_ref[...], wd_ref[...], preferred_element_type=F32)

    @pl.when(f == 0)
    def _():
        cast_side()
        h_ref[...] = _rms(x_ref[...], gpre_ref[...]).astype(BF16)
        o_ref[...] = jnp.zeros_like(o_ref)
        gate_up(a0_ref)

    @pl.when((f > 0) & (f < nf) & ((f & 1) == 1))
    def _():
        gate_up(a1_ref)
        down(a0_ref)

    @pl.when((f > 0) & (f < nf) & ((f & 1) == 0))
    def _():
        gate_up(a0_ref)
        down(a1_ref)

    @pl.when(f == nf)
    def _():
        down(a0_ref if (nf - 1) % 2 == 0 else a1_ref)
        o_ref[...] = x_ref[...] + FFN_RES * _rms(o_ref[...], gpost_ref[...])


def _ffn(x, gpre, w_up, wd, gpost, side=None):
    m = x.shape[0]
    cast = wd.dtype == F32
    tm = min(m, 512 if cast else 1024)
    nf = D_FF // FFN_TF
    row = lambda i, f: (i, 0)
    const = lambda i, f: (0, 0)
    out_specs = [pl.BlockSpec((tm, D_MODEL), row)]
    out_shape = [jax.ShapeDtypeStruct((m, D_MODEL), F32)]
    side_specs, side_args = [], []
    if cast:
        assert side is None
        assert m == tm, "the casting variant writes each weight tile once"
        tf = FFN_TF
        body = _ffn_cast_body
        steps = nf
        up_spec = pl.BlockSpec((D_MODEL, tf), lambda i, f: (0, f))
        up_specs = [up_spec, up_spec]
        up_args = list(w_up)
        down_spec = pl.BlockSpec((tf, D_MODEL), lambda i, f: (f, 0))
        out_specs += [pl.BlockSpec((1, D_MODEL, 2 * tf), lambda i, f: (f, 0, 0)), down_spec]
        out_shape += [jax.ShapeDtypeStruct((nf, D_MODEL, 2 * tf), BF16), jax.ShapeDtypeStruct(wd.shape, BF16)]
        scratch = []
    else:
        tf = FFN_TF
        body = functools.partial(_ffn_skew_body, nf=nf, side=side is not None)
        steps = nf + 1
        up_specs = [pl.BlockSpec((1, D_MODEL, 2 * tf), lambda i, f: (jnp.minimum(f, nf - 1), 0, 0))]
        up_args = [w_up]
        down_spec = pl.BlockSpec((tf, D_MODEL), lambda i, f: (jnp.maximum(f - 1, 0), 0))
        scratch = [pltpu.VMEM((tm, tf), BF16), pltpu.VMEM((tm, tf), BF16)]
        if side is not None:
            w_side, n_side = side
            rs = n_side // (m // tm * steps)
            assert rs * (m // tm * steps) == n_side and rs % 16 == 0
            cols = w_side.shape[2]
            side_specs = [pl.BlockSpec((1, rs, cols), lambda i, f: (0, i * steps + f, 0))]
            side_args = [w_side]
            out_specs.append(pl.BlockSpec((rs, cols), lambda i, f: (i * steps + f, 0)))
            out_shape.append(jax.ShapeDtypeStruct((n_side, cols), BF16))
    limit = VMEM_LIMIT + (4 << 20) if side is not None else VMEM_LIMIT
    return pl.pallas_call(
        body,
        grid=(m // tm, steps),
        in_specs=[pl.BlockSpec((tm, D_MODEL), row), pl.BlockSpec((1, D_MODEL), const),
                  *up_specs, down_spec, pl.BlockSpec((1, D_MODEL), const), *side_specs],
        out_specs=out_specs,
        out_shape=out_shape,
        scratch_shapes=[pltpu.VMEM((tm, D_MODEL), BF16)] + scratch,
        compiler_params=_params(("parallel", "arbitrary"), limit),
        name="ffn_cast" if cast else "ffn",
    )(x, gpre, *up_args, wd, gpost, *side_args)


def _proj_in_body(*refs, cast):
    if cast:
        x_ref, g_ref, w_ref, wgate_ref, proj_ref, gates_ref, w_o, h_ref = refs
    else:
        x_ref, g_ref, w_ref, wgate_ref, proj_ref, gates_ref, h_ref = refs
    nt = (((1,), (1,)), ((), ()))

    @pl.when(pl.program_id(1) == 0)
    def _():
        h = _rms(x_ref[...], g_ref[...]).astype(BF16)
        h_ref[...] = h
        gates_ref[...] = lax.dot_general(h, wgate_ref[...], nt, preferred_element_type=F32)

    if cast:
        w_o[...] = w_ref[0].astype(BF16)
        w_ref = w_o
    proj_ref[...] = lax.dot_general(h_ref[...], w_ref[...], nt, preferred_element_type=F32)


def _proj_in(x, g, w_main_t, w_gate_t):
    m = x.shape[0]
    cast = w_main_t.dtype == F32
    tm = min(m, 1024)
    tn = 1024 if cast else 1536
    if cast:
        assert m == tm, "the casting variant writes each weight tile once"
        w_spec = pl.BlockSpec((1, tn, D_MODEL), lambda i, n: (0, n, 0))
    else:
        w_spec = pl.BlockSpec((tn, D_MODEL), lambda i, n: (n, 0))
    out_specs = [pl.BlockSpec((tm, tn), lambda i, n: (i, n)), pl.BlockSpec((tm, LANES), lambda i, n: (i, 0))]
    out_shape = [jax.ShapeDtypeStruct((m, D_MAIN), F32), jax.ShapeDtypeStruct((m, LANES), F32)]
    if cast:
        out_specs.append(pl.BlockSpec((tn, D_MODEL), lambda i, n: (n, 0)))
        out_shape.append(jax.ShapeDtypeStruct((D_MAIN, D_MODEL), BF16))
    return pl.pallas_call(
        functools.partial(_proj_in_body, cast=cast),
        grid=(m // tm, D_MAIN // tn),
        in_specs=[
            pl.BlockSpec((tm, D_MODEL), lambda i, n: (i, 0)),
            pl.BlockSpec((1, D_MODEL), lambda i, n: (0, 0)),
            w_spec,
            pl.BlockSpec((LANES, D_MODEL), lambda i, n: (0, 0)),
        ],
        out_specs=out_specs,
        out_shape=out_shape,
        scratch_shapes=[pltpu.VMEM((tm, D_MODEL), BF16)],
        compiler_params=_params(("parallel", "arbitrary")),
        name="proj_in_cast" if cast else "proj_in",
    )(x, g, w_main_t, w_gate_t)


def _log_sigmoid(x):
    return jnp.minimum(x, 0.0) - jnp.log1p(jnp.exp(-jnp.abs(x)))


def _seg_cumsum(x, seg_len):
    pos = lax.broadcasted_iota(jnp.int32, x.shape, 0) & (seg_len - 1)
    shift = 1
    while shift < seg_len:
        x = x + jnp.where(pos >= shift, pltpu.roll(x, shift, 0), 0.0)
        shift *= 2
    return x


def _gate_terms(pre, seg_len):
    bt = _seg_cumsum(_log_sigmoid(pre), seg_len)
    return pre, bt, pre.T, bt.T


def _tile_masks(seg_len):
    log2 = seg_len.bit_length() - 1
    t_idx = lax.broadcasted_iota(jnp.int32, (ROWS, ROWS), 0)
    s_idx = lax.broadcasted_iota(jnp.int32, (ROWS, ROWS), 1)
    if seg_len == ROWS:
        return s_idx <= t_idx, None, None
    same = (t_idx >> log2) == (s_idx >> log2)
    last = s_idx == ((t_idx >> log2) << log2) + (seg_len - 1)
    return same & (s_idx <= t_idx), same, last


def _pick(x, j, axis):
    if isinstance(j, int):
        return x[:, j:j + 1] if axis == 1 else x[j:j + 1, :]
    idx = lax.broadcasted_iota(jnp.int32, x.shape, axis)
    return jnp.sum(jnp.where(idx == j, x, 0.0), axis=axis, keepdims=True)


def _mlstm_tile(q, k, v, gates, masks, hd, seg_len, m_prev):
    pre, bt_all, pre_t, bt_t = gates
    valid, same, last = masks
    ig_col = _pick(pre, hd, 1)
    bt_col = _pick(bt_all, hd + N_HEADS, 1)
    key_w = _pick(pre_t, hd, 0) - _pick(bt_t, hd + N_HEADS, 0)

    d = jnp.where(valid, bt_col + key_w, -jnp.inf)
    inter = bt_col + m_prev
    m_t = jnp.maximum(inter, jnp.max(d, axis=1, keepdims=True))
    w_intra = jnp.exp(d - m_t)
    w_inter = jnp.exp(inter - m_t)

    if seg_len == ROWS:
        bt_last = bt_col[ROWS - 1:ROWS, :]
        m_end = m_t[ROWS - 1:ROWS, :]
    else:
        bt_row = _pick(bt_t, hd + N_HEADS, 0)
        bt_last = jnp.sum(jnp.where(last, bt_row, 0.0), axis=1, keepdims=True)
        e = jnp.where(same, bt_last + key_w, -jnp.inf)
        m_end = jnp.maximum(bt_last + m_prev, jnp.max(e, axis=1, keepdims=True))
    g_keys = jnp.exp(bt_last - bt_col + ig_col - m_end)
    g_state = jnp.exp(bt_last + m_prev - m_end)

    ks = k * K_SCALE
    q_bf = q.astype(BF16)
    v_bf = v.astype(BF16)
    s = lax.dot_general(q_bf, ks.astype(BF16), (((1,), (1,)), ((), ())),
                        preferred_element_type=F32) * w_intra
    num = jnp.dot(s.astype(BF16), v_bf, preferred_element_type=F32)
    den = jnp.sum(s, axis=1, keepdims=True)
    kg = ks * g_keys
    return dict(q_bf=q_bf, v_bf=v_bf, num=num, den=den, kg=kg, m_t=m_t, w_inter=w_inter,
                g_state=g_state, m_end=m_end)


def _mlstm_out(t, q, o, q_c, q_n):
    num = t["num"] + q_c * t["w_inter"]
    den = t["den"] + q_n * t["w_inter"]
    h = num / jnp.maximum(jnp.abs(den), jnp.exp(-t["m_t"]))
    return (jax.nn.sigmoid(o) * h).astype(BF16)


def _mix_prompt_body(q_ref, k_ref, v_ref, o_ref, gates_ref, bias_ref, hc_ref, x_ref, wa_ref, wb_ref, g_ref,
                     y_ref, c_out_ref, n_out_ref, m_out_ref, c_ref, n_ref, m_ref, hm_ref, *, n_chunks, nt):
    t = pl.program_id(1)

    def project():
        mix = (jnp.dot(hm_ref[(t - 1) & 1], wa_ref[...], preferred_element_type=F32)
               + jnp.dot(hc_ref[...], wb_ref[...], preferred_element_type=F32))
        y_ref[...] = x_ref[...] + _rms(mix, g_ref[...])

    def recur():
        slot = t & 1
        masks = _tile_masks(ROWS)
        for c in range(n_chunks):
            rows = pl.ds(c * ROWS, ROWS)
            gates = _gate_terms(gates_ref[rows, :] + bias_ref[...], ROWS)
            for hd in range(N_HEADS):
                cols = pl.ds(hd * HEAD_DIM, HEAD_DIM)
                q = q_ref[rows, cols]
                tl = _mlstm_tile(q, k_ref[rows, cols], v_ref[rows, cols], gates, masks, hd, ROWS,
                                 m_ref[hd][:, 0:1])
                c_old = c_ref[hd]
                n_old = n_ref[hd]
                q_c = jnp.dot(tl["q_bf"], c_old.astype(BF16), preferred_element_type=F32)
                q_n = jnp.sum(q * n_old, axis=1, keepdims=True)
                hm_ref[slot, rows, cols] = _mlstm_out(tl, q, o_ref[rows, cols], q_c, q_n)
                g = tl["g_state"]
                c_ref[hd] = g * c_old + lax.dot_general(
                    tl["kg"].astype(BF16), tl["v_bf"], (((0,), (0,)), ((), ())), preferred_element_type=F32)
                n_ref[hd] = g * n_old + jnp.sum(tl["kg"], axis=0, keepdims=True)
                m_ref[hd] = jnp.broadcast_to(tl["m_end"], (1, LANES))

    @pl.when(t == 0)
    def _():
        c_ref[...] = jnp.zeros_like(c_ref)
        n_ref[...] = jnp.zeros_like(n_ref)
        m_ref[...] = jnp.zeros_like(m_ref)
        recur()

    @pl.when((t > 0) & (t < nt))
    def _():
        project()
        recur()

    @pl.when(t == nt)
    def _():
        project()
        c_out_ref[0] = c_ref[...]
        n_out_ref[0] = n_ref[...]
        m_out_ref[0] = m_ref[...]


def _mix_prompt(proj, gates, bias, hc, x, w_out, g, batch, seq, *, tt=256):
    nt = seq // tt
    cur = lambda j: (lambda b, t: (b * nt + jnp.minimum(t, nt - 1), j))
    prev = lambda b, t: (b * nt + jnp.maximum(t - 1, 0), 0)
    state = lambda b, t: (b, 0, 0, 0)
    const = lambda b, t: (0, 0)
    return pl.pallas_call(
        functools.partial(_mix_prompt_body, n_chunks=tt // ROWS, nt=nt),
        grid=(batch, nt + 1),
        in_specs=[
            pl.BlockSpec((tt, D_MLSTM), cur(0)),
            pl.BlockSpec((tt, D_MLSTM), cur(1)),
            pl.BlockSpec((tt, D_MLSTM), cur(2)),
            pl.BlockSpec((tt, D_MLSTM), cur(3)),
            pl.BlockSpec((tt, LANES), cur(0)),
            pl.BlockSpec((1, LANES), const),
            pl.BlockSpec((tt, D_CONV), prev),
            pl.BlockSpec((tt, D_MODEL), prev),
            pl.BlockSpec((D_MLSTM, D_MODEL), lambda b, t: (0, 0)),
            pl.BlockSpec((D_CONV, D_MODEL), lambda b, t: (1, 0)),
            pl.BlockSpec((1, D_MODEL), const),
        ],
        out_specs=[
            pl.BlockSpec((tt, D_MODEL), prev),
            pl.BlockSpec((1, N_HEADS, HEAD_DIM, HEAD_DIM), state),
            pl.BlockSpec((1, N_HEADS, 1, HEAD_DIM), state),
            pl.BlockSpec((1, N_HEADS, 1, LANES), state),
        ],
        out_shape=[
            jax.ShapeDtypeStruct((batch * seq, D_MODEL), F32),
            jax.ShapeDtypeStruct((batch, N_HEADS, HEAD_DIM, HEAD_DIM), F32),
            jax.ShapeDtypeStruct((batch, N_HEADS, 1, HEAD_DIM), F32),
            jax.ShapeDtypeStruct((batch, N_HEADS, 1, LANES), F32),
        ],
        scratch_shapes=[pltpu.VMEM((N_HEADS, HEAD_DIM, HEAD_DIM), F32), pltpu.VMEM((N_HEADS, 1, HEAD_DIM), F32),
                        pltpu.VMEM((N_HEADS, 1, LANES), F32), pltpu.VMEM((2, tt, D_MLSTM), BF16)],
        compiler_params=_params(("parallel", "arbitrary")),
        name="mix_prompt",
    )(proj, proj, proj, proj, gates, bias, hc, x, w_out, w_out, g)


def _mlstm_sample_body(q_ref, k_ref, v_ref, o_ref, gates_ref, bias_ref, mrow_ref, c_ref, n_ref,
                       hm_ref, c_out_ref, n_out_ref, m_out_ref, *, seg_len):
    hd = pl.program_id(1)
    n_seg = ROWS // seg_len
    grp = 16 // seg_len
    log2 = seg_len.bit_length() - 1
    q = q_ref[...]
    gates = _gate_terms(gates_ref[...] + bias_ref[...], seg_len)
    t = _mlstm_tile(q, k_ref[...], v_ref[...], gates, _tile_masks(seg_len), hd, seg_len, mrow_ref[0])

    seg_of_row = lax.broadcasted_iota(jnp.int32, (16, 1), 0) >> log2
    qc_parts, n_parts = [], []
    for j in range(ROWS // 16):
        qg = t["q_bf"][16 * j:16 * (j + 1)]
        qc, nr = None, None
        for i in range(grp):
            b = grp * j + i
            r = jnp.dot(qg, c_ref[0, b, 0].astype(BF16), preferred_element_type=F32)
            nb = jnp.broadcast_to(n_ref[b, 0], (16, HEAD_DIM))
            qc = r if i == 0 else jnp.where(seg_of_row == i, r, qc)
            nr = nb if i == 0 else jnp.where(seg_of_row == i, nb, nr)
        qc_parts.append(qc)
        n_parts.append(nr)
    q_c = jnp.concatenate(qc_parts, axis=0)
    q_n = jnp.sum(q * jnp.concatenate(n_parts, axis=0), axis=1, keepdims=True)
    hm_ref[...] = _mlstm_out(t, q, o_ref[...], q_c, q_n)

    kg = t["kg"]
    kg_t = kg.T
    seg_of_lane = lax.broadcasted_iota(jnp.int32, (1, ROWS), 1) >> log2
    seg_of_row8 = lax.broadcasted_iota(jnp.int32, (8, 1), 0) >> log2
    per8 = 8 // seg_len
    for b in range(n_seg):
        g = t["g_state"][seg_len * b:seg_len * b + 1, :]
        upd = jnp.dot(jnp.where(seg_of_lane == b, kg_t, 0.0).astype(BF16), t["v_bf"],
                      preferred_element_type=F32)
        c_out_ref[0, b, 0] = g * c_ref[0, b, 0] + upd
        kg8 = kg[8 * (b // per8):8 * (b // per8) + 8]
        n_out_ref[b, 0] = g * n_ref[b, 0] + jnp.sum(
            jnp.where(seg_of_row8 == (b % per8), kg8, 0.0), axis=0, keepdims=True)
    m_out_ref[0] = t["m_end"]


def _mlstm_sample(proj, gates, bias, m_rows, c0, n0, seg_len):
    m = proj.shape[0]
    n_seg = ROWS // seg_len
    col = lambda j: (lambda i, h: (i, j * N_HEADS + h))
    return pl.pallas_call(
        functools.partial(_mlstm_sample_body, seg_len=seg_len),
        grid=(m // ROWS, N_HEADS),
        in_specs=[
            pl.BlockSpec((ROWS, HEAD_DIM), col(0)),
            pl.BlockSpec((ROWS, HEAD_DIM), col(1)),
            pl.BlockSpec((ROWS, HEAD_DIM), col(2)),
            pl.BlockSpec((ROWS, HEAD_DIM), col(3)),
            pl.BlockSpec((ROWS, LANES), lambda i, h: (i, 0)),
            pl.BlockSpec((1, LANES), lambda i, h: (0, 0)),
            pl.BlockSpec((1, ROWS, 1), lambda i, h: (h, i, 0)),
            pl.BlockSpec((1, n_seg, 1, HEAD_DIM, HEAD_DIM), lambda i, h: (0, i, h, 0, 0)),
            pl.BlockSpec((n_seg, 1, 1, HEAD_DIM), lambda i, h: (i, h, 0, 0)),
        ],
        out_specs=[
            pl.BlockSpec((ROWS, HEAD_DIM), lambda i, h: (i, h)),
            pl.BlockSpec((1, n_seg, 1, HEAD_DIM, HEAD_DIM), lambda i, h: (0, i, h, 0, 0)),
            pl.BlockSpec((n_seg, 1, 1, HEAD_DIM), lambda i, h: (i, h, 0, 0)),
            pl.BlockSpec((1, ROWS, 1), lambda i, h: (h, i, 0)),
        ],
        out_shape=[
            jax.ShapeDtypeStruct((m, D_MLSTM), BF16),
            jax.ShapeDtypeStruct(c0.shape, F32),
            jax.ShapeDtypeStruct(n0.shape, F32),
            jax.ShapeDtypeStruct(m_rows.shape, F32),
        ],
        compiler_params=_params(("parallel", "parallel")),
        name="mlstm_sample",
    )(proj, proj, proj, proj, gates, bias, m_rows, c0, n0)


def _ln_swish(y, g, b):
    mu = jnp.mean(y, axis=-1, keepdims=True)
    yc = y - mu
    var = jnp.mean(yc * yc, axis=-1, keepdims=True)
    z = yc * lax.rsqrt(var + EPS) * g + b
    return z * jax.nn.sigmoid(z)


def _conv_prompt_body(a_ref, b_ref, w_ref, bdw_ref, gln_ref, bln_ref, wg_ref, wu_ref, wd_ref, wo_ref,
                      hc_ref, st_ref, wgu_o, wd_o, wo_o, u_ref, us_ref, wb_ref, y_ref, *, tt):
    t_id = pl.program_id(1)
    n_shift = tt + CONV_PAD - 8

    @pl.when(t_id == 0)
    def _():
        u_ref[0:CONV_PAD, :] = jnp.zeros((CONV_PAD, D_CONV), F32)
        for s in range(CONV_WIDTH):
            wb_ref[s] = jnp.broadcast_to(w_ref[s:s + 1, :], (8, D_CONV))

    for c in range(D_FF // FFN_TF):
        wgu_o[c, :, :FFN_TF] = wg_ref[:, c * FFN_TF:(c + 1) * FFN_TF].astype(BF16)
        wgu_o[c, :, FFN_TF:] = wu_ref[:, c * FFN_TF:(c + 1) * FFN_TF].astype(BF16)
    wd_o[...] = wd_ref[...].astype(BF16)
    wo_o[...] = wo_ref[...].astype(BF16)

    u_ref[CONV_PAD:CONV_PAD + tt, :] = a_ref[...] * jax.nn.sigmoid(b_ref[...])
    for r in range(1, 8):
        us_ref[r - 1] = u_ref[r:r + n_shift, :]

    def row_block(i, carry):
        base = pl.multiple_of(i * CONV_ROWS, CONV_ROWS)
        n_slab = CONV_ROWS // 8
        acc = [jnp.broadcast_to(bdw_ref[...], (8, D_CONV))] * n_slab
        for s in range(CONV_WIDTH):
            k8, r = divmod(CONV_PAD - HALO + s, 8)
            w = wb_ref[s]
            for j in range(n_slab):
                rows = pl.ds(pl.multiple_of(base + 8 * (k8 + j), 8), 8)
                win = u_ref[rows, :] if r == 0 else us_ref[r - 1, rows, :]
                acc[j] = acc[j] + w * win
        for j in range(n_slab):
            y_ref[pl.ds(pl.multiple_of(base + 8 * j, 8), 8), :] = acc[j]
        return carry

    lax.fori_loop(0, tt // CONV_ROWS, row_block, 0)
    hc_ref[...] = _ln_swish(y_ref[...], gln_ref[...], bln_ref[...]).astype(BF16)

    @pl.when(t_id == pl.num_programs(1) - 1)
    def _():
        st_ref[0, 0] = u_ref[CONV_PAD + tt - HALO:CONV_PAD + tt, :]

    u_ref[0:CONV_PAD, :] = u_ref[tt:tt + CONV_PAD, :]


def _conv_prompt(proj, w_dw, b_dw, g_ln, b_ln, w_gate, w_up, w_down, w_out, batch, seq, *, tt=256):
    nt = seq // tt
    steps = batch * nt
    nf = D_FF // FFN_TF
    r_up, r_dn = D_MODEL // steps, D_FF // steps
    assert r_up * steps == D_MODEL and r_dn * steps == D_FF and r_up % 16 == 0 and r_dn % 16 == 0
    assert w_out.shape == (D_MODEL, D_MODEL)
    const = lambda b, t: (0, 0)
    slab = lambda b, t: (b * nt + t, 0)
    return pl.pallas_call(
        functools.partial(_conv_prompt_body, tt=tt),
        grid=(batch, nt),
        in_specs=[
            pl.BlockSpec((tt, D_CONV), lambda b, t: (b * nt + t, 4)),
            pl.BlockSpec((tt, D_CONV), lambda b, t: (b * nt + t, 5)),
            pl.BlockSpec((CONV_WIDTH, D_CONV), const),
            pl.BlockSpec((1, D_CONV), const),
            pl.BlockSpec((1, D_CONV), const),
            pl.BlockSpec((1, D_CONV), const),
            pl.BlockSpec((r_up, D_FF), slab),
            pl.BlockSpec((r_up, D_FF), slab),
            pl.BlockSpec((r_dn, D_MODEL), slab),
            pl.BlockSpec((r_up, D_MODEL), slab),
        ],
        out_specs=[
            pl.BlockSpec((tt, D_CONV), lambda b, t: (b * nt + t, 0)),
            pl.BlockSpec((1, 1, HALO, D_CONV), lambda b, t: (0, b, 0, 0)),
            pl.BlockSpec((nf, r_up, 2 * FFN_TF), lambda b, t: (0, b * nt + t, 0)),
            pl.BlockSpec((r_dn, D_MODEL), slab),
            pl.BlockSpec((r_up, D_MODEL), slab),
        ],
        out_shape=[
            jax.ShapeDtypeStruct((batch * seq, D_CONV), BF16),
            jax.ShapeDtypeStruct((1, batch, HALO, D_CONV), F32),
            jax.ShapeDtypeStruct((nf, D_MODEL, 2 * FFN_TF), BF16),
            jax.ShapeDtypeStruct((D_FF, D_MODEL), BF16),
            jax.ShapeDtypeStruct((D_MODEL, D_MODEL), BF16),
        ],
        scratch_shapes=[pltpu.VMEM((tt + CONV_PAD, D_CONV), F32),
                        pltpu.VMEM((7, tt + CONV_PAD - 8, D_CONV), F32),
                        pltpu.VMEM((CONV_WIDTH, 8, D_CONV), F32),
                        pltpu.VMEM((tt, D_CONV), F32)],
        compiler_params=_params(("parallel", "arbitrary")),
        name="conv_prompt",
    )(proj, proj, w_dw, b_dw, g_ln, b_ln, w_gate, w_up, w_down, w_out)


def _conv_sample_body(a_ref, b_ref, st_ref, w_ref, bdw_ref, gln_ref, bln_ref, hc_ref, st_out_ref,
                      wb_ref, *, seq):
    @pl.when(pl.program_id(0) == 0)
    def _():
        for s in range(CONV_WIDTH):
            wb_ref[s] = jnp.broadcast_to(w_ref[s:s + 1, :], (8, D_CONV))

    acc = [jnp.broadcast_to(bdw_ref[...], (8, D_CONV))] * seq
    for j in range(HALO + seq):
        slab = st_ref[0, j] if j < HALO else a_ref[j - HALO] * jax.nn.sigmoid(b_ref[j - HALO])
        for t in range(seq):
            if 0 <= j - t < CONV_WIDTH:
                acc[t] = acc[t] + wb_ref[j - t] * slab
        if j >= seq:
            st_out_ref[0, j - seq] = slab
    for t in range(seq):
        hc_ref[t] = _ln_swish(acc[t], gln_ref[...], bln_ref[...])


def _conv_sample(a_t, b_t, state_t, w_dw, b_dw, g_ln, b_ln):
    seq, batch, _ = a_t.shape
    bb = 8
    const = lambda i: (0, 0)
    tok = pl.BlockSpec((seq, bb, D_CONV), lambda i: (0, i, 0))
    hist = pl.BlockSpec((1, HALO, bb, D_CONV), lambda i: (0, 0, i, 0))
    return pl.pallas_call(
        functools.partial(_conv_sample_body, seq=seq),
        grid=(batch // bb,),
        in_specs=[tok, tok, hist,
                  pl.BlockSpec((CONV_WIDTH, D_CONV), const),
                  pl.BlockSpec((1, D_CONV), const),
                  pl.BlockSpec((1, D_CONV), const),
                  pl.BlockSpec((1, D_CONV), const)],
        out_specs=[tok, hist],
        out_shape=[
            jax.ShapeDtypeStruct(a_t.shape, F32),
            jax.ShapeDtypeStruct(state_t.shape, F32),
        ],
        scratch_shapes=[pltpu.VMEM((CONV_WIDTH, 8, D_CONV), F32)],
        compiler_params=_params(("arbitrary",)),
        name="conv_sample",
    )(a_t, b_t, state_t, w_dw, b_dw, g_ln, b_ln)


def _proj_out_body(hm_ref, hc_ref, x_ref, wa_ref, wb_ref, g_ref, o_ref):
    mix = (jnp.dot(hm_ref[...], wa_ref[...], preferred_element_type=F32)
           + jnp.dot(hc_ref[...].astype(BF16), wb_ref[...], preferred_element_type=F32))
    o_ref[...] = x_ref[...] + _rms(mix, g_ref[...])


def _proj_out(hm, hc, x, w_out, g, *, tm=512):
    m = x.shape[0]
    row = lambda i: (i, 0)
    return pl.pallas_call(
        _proj_out_body,
        grid=(m // tm,),
        in_specs=[
            pl.BlockSpec((tm, D_MLSTM), row),
            pl.BlockSpec((tm, D_CONV), row),
            pl.BlockSpec((tm, D_MODEL), row),
            pl.BlockSpec((D_MLSTM, D_MODEL), lambda i: (0, 0)),
            pl.BlockSpec((D_CONV, D_MODEL), lambda i: (1, 0)),
            pl.BlockSpec((1, D_MODEL), lambda i: (0, 0)),
        ],
        out_specs=pl.BlockSpec((tm, D_MODEL), row),
        out_shape=jax.ShapeDtypeStruct((m, D_MODEL), F32),
        compiler_params=_params(("parallel",)),
        name="proj_out",
    )(hm, hc, x, w_out, w_out, g)


def kernel(x_prompt, x_sample, state_mlstm_C, state_mlstm_n, state_mlstm_m, state_conv, g_ffn1_pre, w_ffn1_gate, w_ffn1_up, w_ffn1_down, g_ffn1_post, g_mix_pre, w_in, b_igate, b_fgate, w_dw, b_dw, g_conv_ln, b_conv_ln, w_out, g_mix_post, g_ffn2_pre, w_ffn2_gate, w_ffn2_up, w_ffn2_down, g_ffn2_post):
    depth = state_mlstm_C.shape[0]
    assert depth == 1, "kernel handles a single layer"
    bp, tp, _ = x_prompt.shape
    bs, ts, _ = x_sample.shape
    l = 0

    w_in_t = jnp.swapaxes(w_in, 1, 2)
    w_gate_t = jnp.pad(w_in_t[l, D_MAIN:], ((0, LANES - 2 * N_HEADS), (0, 0))).astype(BF16)
    bias = jnp.pad(jnp.concatenate([b_igate[l], b_fgate[l]]), (0, LANES - 2 * N_HEADS))[None, :]

    xs, *ffn1 = _ffn(x_sample.reshape(bs * ts, D_MODEL), g_ffn1_pre,
                     (w_ffn1_gate[l], w_ffn1_up[l]), w_ffn1_down[l], g_ffn1_post)
    xp, w_main_t = _ffn(x_prompt.reshape(bp * tp, D_MODEL), g_ffn1_pre, *ffn1, g_ffn1_post,
                        side=(w_in_t, D_MAIN))
    proj_s, gates_s = _proj_in(xs, g_mix_pre, w_main_t, w_gate_t)
    proj_p, gates_p = _proj_in(xp, g_mix_pre, w_main_t, w_gate_t)
    hc_p, conv_p, *ffn2, w_o = _conv_prompt(proj_p, w_dw[l], b_dw, g_conv_ln, b_conv_ln,
                                            w_ffn2_gate[l], w_ffn2_up[l], w_ffn2_down[l], w_out[l], bp, tp)

    m_rows = jnp.repeat(state_mlstm_m[l].T, ts, axis=1)[:, :, None]
    hm_s, c_s, n_s, m_s = _mlstm_sample(
        proj_s, gates_s, bias, m_rows, state_mlstm_C, state_mlstm_n[l][:, :, None, :], ts)
    glu_t = jnp.swapaxes(proj_s[:, 4 * D_MLSTM:].reshape(bs, ts, 2, D_CONV), 0, 1)
    hc_t, conv_t = _conv_sample(glu_t[:, :, 0], glu_t[:, :, 1], jnp.swapaxes(state_conv, 1, 2),
                                w_dw[l], b_dw, g_conv_ln, b_conv_ln)
    hc_s = jnp.swapaxes(hc_t, 0, 1).reshape(bs * ts, D_CONV)
    xs = _proj_out(hm_s, hc_s, xs, w_o, g_mix_post)
    ys = _ffn(xs, g_ffn2_pre, *ffn2, g_ffn2_post)[0].reshape(bs, ts, D_MODEL)

    xp, c_p, n_p, m_p = _mix_prompt(proj_p, gates_p, bias, hc_p, xp, w_o, g_mix_post, bp, tp)
    yp = _ffn(xp, g_ffn2_pre, *ffn2, g_ffn2_post)[0].reshape(bp, tp, D_MODEL)

    return (yp, ys,
            c_p[None], n_p[:, :, 0, :][None], m_p[:, :, 0, 0][None], conv_p,
            c_s, n_s[:, :, 0, :][None], m_s[:, ::ts, 0].T[None], jnp.swapaxes(conv_t, 1, 2))
```
